```python
import math
import jax
import jax.numpy as jnp
from jax import lax
import numpy as np

D_MODEL = 2048
BATCH = 32
SEQ = 256
DEPTH = 2
DEC_BATCH = 8
DEC_SEQ = 4096
PAST_LEN = 512

GRID_W = 64
N_EVEN = (DEPTH + 1) // 2
N_ODD = DEPTH // 2
EPS = 1e-6
NEG_INF = -1e30

A_WIDTH = D_MODEL // 2
A_SCONV = 3
A_BANDS = 16
A_EMB = 2 * A_BANDS + 1
A_FFN = 64
A_SHORT_DECAY_PCT = 0.3
A_LONG_DECAY_PCT = 1.5
A_DECAY_TARGET = 1e-2

B_HEADS = 8
HEAD_DIM = 128
B_KV_HEADS = 2
B_GROUP = B_HEADS // B_KV_HEADS
B_WIDTH = B_HEADS * HEAD_DIM
WINDOW = 128
BLOCK = 128
BAND = BLOCK + 2 * WINDOW
ROPE_BASE = 10000.0
EVEN_IN = 3 * A_WIDTH + (B_HEADS + 2 * B_KV_HEADS) * HEAD_DIM
EVEN_OUT = A_WIDTH + B_WIDTH

C_HEADS = 8
C_DK = D_MODEL // C_HEADS
C_DV = 2 * D_MODEL // C_HEADS
C_CHUNK = 128
ODD_IN = 2 * C_HEADS * C_DK + 2 * C_HEADS * C_DV
ODD_OUT = C_HEADS * C_DV

N_EXPERTS = 16
EXPERT_FF = 2048
CAPACITY_FACTOR = 2

kernel_name = 'hybrid_diffusion_hyena_swa_retention_ec_moe_step'


def rmsnorm(x, g):
    xf = x.astype(jnp.float32)
    y = xf * lax.rsqrt(jnp.mean(xf * xf, axis=-1, keepdims=True) + EPS)
    return (y * g.astype(jnp.float32)).astype(x.dtype)


def adaln(cond, w, b):
    m = jax.nn.silu(cond) @ w + b
    return jnp.split(m[:, None, :], 6, axis=-1)


def short_conv(x, w, b):
    L = x.shape[1]
    pad = A_SCONV // 2
    xp = jnp.pad(x, ((0, 0), (pad, A_SCONV - 1 - pad), (0, 0)))
    return sum(xp[:, i:i + L] * w[i] for i in range(A_SCONV)) + b


def hyena_filters(L, w1, b1, w2, b2, w3, freq):
    f32 = jnp.float32
    t = jnp.arange(L, dtype=f32) / L
    bands = jnp.linspace(1e-4, A_BANDS - 1, A_BANDS, dtype=f32)
    ang = 2.0 * math.pi * t[:, None] * bands[None, :]
    feats = jnp.concatenate([t[:, None], jnp.cos(ang), -jnp.sin(ang)], axis=-1)
    h = jnp.sin(freq[0].astype(f32) * (feats @ w1.astype(f32) + b1.astype(f32)))
    h = jnp.sin(freq[1].astype(f32) * (h @ w2.astype(f32) + b2.astype(f32)))
    h = (h @ w3.astype(f32)).reshape(L, 2, A_WIDTH)
    max_decay = math.log(A_DECAY_TARGET) / A_SHORT_DECAY_PCT
    min_decay = math.log(A_DECAY_TARGET) / A_LONG_DECAY_PCT
    deltas = jnp.abs(jnp.linspace(min_decay, max_decay, A_WIDTH, dtype=f32))
    h = h * jnp.exp(-t[:, None] * deltas[None, :])[:, None, :]
    h = h / (jnp.sum(jnp.abs(h), axis=(0, 1), keepdims=True) + EPS)
    return h[:, 0], h[:, 1]


def long_conv(z, h_fwd, h_bwd, bias):
    L = z.shape[1]
    taps = jnp.concatenate([h_fwd, jnp.zeros((1, A_WIDTH), jnp.float32), h_bwd[:0:-1]], axis=0)
    zf = z.astype(jnp.float32)
    y = jnp.fft.irfft(jnp.fft.rfft(zf, n=2 * L, axis=1) * jnp.fft.rfft(taps, axis=0)[None], n=2 * L, axis=1)[:, :L]
    return (y + zf * bias.astype(jnp.float32)).astype(z.dtype)


def rope_2d(x):
    L = x.shape[1]
    rows = L // GRID_W
    row = jnp.repeat(jnp.arange(rows), GRID_W).astype(jnp.float32)
    col = jnp.tile(jnp.arange(GRID_W), rows).astype(jnp.float32)
    half = HEAD_DIM // 2
    nf = half // 2
    inv = jnp.exp(-math.log(ROPE_BASE) * jnp.arange(nf, dtype=jnp.float32) / nf)
    xf = x.astype(jnp.float32)

    def rot(xh, pos):
        ang = pos[:, None] * inv[None, :]
        cos = jnp.cos(ang)[None, :, None, :]
        sin = jnp.sin(ang)[None, :, None, :]
        x1, x2 = xh[..., :nf], xh[..., nf:]
        return jnp.concatenate([x1 * cos - x2 * sin, x1 * sin + x2 * cos], axis=-1)

    return jnp.concatenate([rot(xf[..., :half], row), rot(xf[..., half:], col)], axis=-1).astype(x.dtype)


def sink_softmax(logits, sink):
    s = jnp.broadcast_to(sink.astype(jnp.float32)[None, :, :, None, None], logits.shape[:-1] + (1,))
    p = jax.nn.softmax(jnp.concatenate([logits, s], axis=-1), axis=-1)
    return p[..., :-1]


def attn_context(q, k, v, sink):
    B, L = q.shape[:2]
    nb = L // BLOCK
    scale = HEAD_DIM ** -0.5
    qb = jnp.moveaxis(q.reshape(B, nb, BLOCK, B_KV_HEADS, B_GROUP, HEAD_DIM), 1, 0)

    def body(qblk):
        s = jnp.einsum('bqkgd,bskd->bkgqs', qblk, k).astype(jnp.float32) * scale
        p = sink_softmax(s, sink).astype(v.dtype)
        return jnp.einsum('bkgqs,bskd->bqkgd', p, v)

    o = lax.map(body, qb)
    return jnp.moveaxis(o, 0, 1).reshape(B, L, B_WIDTH)


def attn_latent(q, k, v, ck, cv, sink):
    B, L = q.shape[:2]
    nb = L // BLOCK
    scale = HEAD_DIM ** -0.5
    qb = jnp.moveaxis(q.reshape(B, nb, BLOCK, B_KV_HEADS, B_GROUP, HEAD_DIM), 1, 0)
    kp = jnp.pad(k, ((0, 0), (WINDOW, WINDOW), (0, 0), (0, 0)))
    vp = jnp.pad(v, ((0, 0), (WINDOW, WINDOW), (0, 0), (0, 0)))
    a_idx = jnp.arange(BLOCK)[:, None]
    b_idx = jnp.arange(BAND)[None, :]
    band_ok = (b_idx >= a_idx) & (b_idx <= a_idx + 2 * WINDOW)

    def body(args):
        qblk, i = args
        kb = lax.dynamic_slice_in_dim(kp, i * BLOCK, BAND, axis=1)
        vb = lax.dynamic_slice_in_dim(vp, i * BLOCK, BAND, axis=1)
        pos = i * BLOCK - WINDOW + jnp.arange(BAND)
        ok = band_ok & ((pos >= 0) & (pos < L))[None, :]
        sb = jnp.einsum('bqkgd,bskd->bkgqs', qblk, kb).astype(jnp.float32) * scale
        sb = jnp.where(ok, sb, NEG_INF)
        sc = jnp.einsum('bqkgd,bskd->bkgqs', qblk, ck).astype(jnp.float32) * scale
        p = sink_softmax(jnp.concatenate([sb, sc], axis=-1), sink).astype(v.dtype)
        return (jnp.einsum('bkgqs,bskd->bqkgd', p[..., :BAND], vb)
                + jnp.einsum('bkgqs,bskd->bqkgd', p[..., BAND:], cv))

    o = lax.map(body, (qb, jnp.arange(nb)))
    return jnp.moveaxis(o, 0, 1).reshape(B, L, B_WIDTH)


def even_mixer(h, w_in, w_out, sc_w, sc_b, f_w1, f_b1, f_w2, f_b2, f_w3, f_freq, f_bias,
               qn, kn, sink, ctx_k=None, ctx_v=None):
    B, L, _ = h.shape
    s1 = 3 * A_WIDTH
    s2 = s1 + B_WIDTH
    s3 = s2 + B_KV_HEADS * HEAD_DIM
    hy, q, k, v = jnp.split(h @ w_in, [s1, s2, s3], axis=-1)
    hy = short_conv(hy, sc_w, sc_b)
    hv, x0, x1 = jnp.split(hy, 3, axis=-1)
    h_f, h_b = hyena_filters(L, f_w1, f_b1, f_w2, f_b2, f_w3, f_freq)
    ya = x0 * long_conv(x1 * hv, h_f, h_b, f_bias)
    q = rmsnorm(q.reshape(B, L, B_HEADS, HEAD_DIM), qn)
    k = rmsnorm(k.reshape(B, L, B_KV_HEADS, HEAD_DIM), kn)
    v = v.reshape(B, L, B_KV_HEADS, HEAD_DIM)
    sink_g = sink.reshape(B_KV_HEADS, B_GROUP)
    if ctx_k is None:
        yb = attn_context(q, k, v, sink_g)
        kv = (k, v)
    else:
        yb = attn_latent(rope_2d(q), rope_2d(k), v, ctx_k, ctx_v, sink_g)
        kv = None
    return jnp.concatenate([ya, yb], axis=-1) @ w_out, kv


def retention_scan(q, k, v, log_gamma, s0):
    B, L = q.shape[:2]
    nc = L // C_CHUNK
    idx = jnp.arange(C_CHUNK, dtype=jnp.float32)
    diff = idx[:, None] - idx[None, :]
    dmat = jnp.where(diff >= 0, jnp.exp(log_gamma[:, None, None] * jnp.maximum(diff, 0.0)), 0.0)
    xi = jnp.exp(log_gamma[None, :] * (idx[:, None] + 1.0))
    zeta = jnp.exp(log_gamma[None, :] * (C_CHUNK - 1.0 - idx[:, None]))
    chunk_decay = jnp.exp(log_gamma * C_CHUNK)

    def to_chunks(a):
        return jnp.moveaxis(a.reshape((B, nc, C_CHUNK) + a.shape[2:]), 1, 0)

    def step(s, inp):
        qc, kc, vc = inp
        scores = jnp.einsum('bihd,bjhd->bhij', qc, kc) * dmat
        o = (jnp.einsum('bhij,bjhe->bihe', scores, vc)
             + jnp.einsum('bihd,bhde->bihe', qc, s) * xi[None, :, :, None])
        s = s * chunk_decay[None, :, None, None] + jnp.einsum('bjhd,bjhe->bhde', kc * zeta[None, :, :, None], vc)
        return s, o

    s_fin, o = lax.scan(step, s0, (to_chunks(q), to_chunks(k), to_chunks(v)))
    return jnp.moveaxis(o, 0, 1).reshape((B, L) + v.shape[2:]), s_fin


def odd_mixer(h, w_in, w_out, decay, gn, s_f0, s_b0):
    f32 = jnp.float32
    B, L, _ = h.shape
    nqk = C_HEADS * C_DK
    q, k, v, g = jnp.split(h @ w_in, [nqk, 2 * nqk, 2 * nqk + C_HEADS * C_DV], axis=-1)
    q = q.reshape(B, L, C_HEADS, C_DK).astype(f32)
    k = k.reshape(B, L, C_HEADS, C_DK).astype(f32) * (C_DK ** -0.5)
    v = v.reshape(B, L, C_HEADS, C_DV).astype(f32)
    log_gamma = -jnp.exp(decay.astype(f32))
    of, sf = retention_scan(q, k, v, log_gamma[0], s_f0.astype(f32))
    ob, sb = retention_scan(q[:, ::-1], k[:, ::-1], v[:, ::-1], log_gamma[1], s_b0.astype(f32))
    o = of + ob[:, ::-1]
    mu = jnp.mean(o, axis=-1, keepdims=True)
    var = jnp.mean(jnp.square(o - mu), axis=-1, keepdims=True)
    o = (o - mu) * lax.rsqrt(var + EPS) * gn.astype(f32)
    o = o.reshape(B, L, ODD_OUT).astype(h.dtype)
    return (jax.nn.silu(g) * o) @ w_out, sf, sb


def ec_moe(h, router, w1, w3, w2):
    B, L, D = h.shape
    cap = CAPACITY_FACTOR * L // N_EXPERTS
    aff = jax.nn.softmax((h @ router).astype(jnp.float32), axis=-1)
    gate, idx = lax.top_k(jnp.swapaxes(aff, 1, 2), cap)

    def per_set(args):
        hb, ib, gb = args
        xs = hb[ib]
        a = jnp.einsum('ecd,edf->ecf', xs, w1)
        b = jnp.einsum('ecd,edf->ecf', xs, w3)
        y = jnp.einsum('ecf,efd->ecd', jax.nn.silu(a) * b, w2) * gb[..., None].astype(h.dtype)
        return jnp.zeros((L, D), h.dtype).at[ib.reshape(-1)].add(y.reshape(-1, D))

    return lax.map(per_set, (h, idx, gate))


def setup_inputs(seed: int = 0) -> dict:
    key = jax.random.key(seed)
    ks = iter(jax.random.split(key, 40))
    d = D_MODEL

    def nrm(shape, scale=1.0):
        return jax.random.normal(next(ks), shape, jnp.float32) * scale

    def gain(shape):
        return 1.0 + nrm(shape, 0.05)

    gammas = 1.0 - 2.0 ** (-5.0 - jnp.arange(C_HEADS, dtype=jnp.float32))
    decay_base = jnp.log(-jnp.log(gammas))
    return {
        'x_prompt': nrm((BATCH, SEQ, d)),
        'x_sample': nrm((DEC_BATCH, DEC_SEQ, d)),
        'cache_attn_k': nrm((DEC_BATCH, N_EVEN, PAST_LEN, B_KV_HEADS, HEAD_DIM)),
        'cache_attn_v': nrm((DEC_BATCH, N_EVEN, PAST_LEN, B_KV_HEADS, HEAD_DIM)),
        'state_ret_fwd': nrm((DEC_BATCH, N_ODD, C_HEADS, C_DK, C_DV)),
        'state_ret_bwd': nrm((DEC_BATCH, N_ODD, C_HEADS, C_DK, C_DV)),
        'c': nrm((DEC_BATCH, d)),
        'c_ctx': nrm((d,)),
        'ada_w': nrm((DEPTH, d, 6 * d), 0.5 * d ** -0.5),
        'ada_b': nrm((DEPTH, 6 * d), 0.02),
        'norm1_g': gain((DEPTH, d)),
        'norm2_g': gain((DEPTH, d)),
        'ev_w_in': nrm((N_EVEN, d, EVEN_IN), d ** -0.5),
        'ev_w_out': nrm((N_EVEN, EVEN_OUT, d), EVEN_OUT ** -0.5),
        'hy_sconv_w': nrm((N_EVEN, A_SCONV, 3 * A_WIDTH), A_SCONV ** -0.5),
        'hy_sconv_b': nrm((N_EVEN, 3 * A_WIDTH), 0.02),
        'hy_ffn_w1': nrm((N_EVEN, A_EMB, A_FFN), A_EMB ** -0.5),
        'hy_ffn_b1': nrm((N_EVEN, A_FFN), 0.02),
        'hy_ffn_w2': nrm((N_EVEN, A_FFN, A_FFN), A_FFN ** -0.5),
        'hy_ffn_b2': nrm((N_EVEN, A_FFN), 0.02),
        'hy_ffn_w3': nrm((N_EVEN, A_FFN, 2 * A_WIDTH), A_FFN ** -0.5),
        'hy_freq': gain((N_EVEN, 2, A_FFN)),
        'hy_bias': nrm((N_EVEN, A_WIDTH), 0.5),
        'at_q_norm': gain((N_EVEN, HEAD_DIM)),
        'at_k_norm': gain((N_EVEN, HEAD_DIM)),
        'at_sink': nrm((N_EVEN, B_HEADS), 1.0),
        'od_w_in': nrm((N_ODD, d, ODD_IN), d ** -0.5),
        'od_w_out': nrm((N_ODD, ODD_OUT, d), ODD_OUT ** -0.5),
        'ret_decay': decay_base[None, None, :] + nrm((N_ODD, 2, C_HEADS), 0.05),
        'ret_gn': gain((N_ODD, C_HEADS, C_DV)),
        'moe_router': nrm((DEPTH, d, N_EXPERTS), d ** -0.5),
        'moe_w1': nrm((DEPTH, N_EXPERTS, d, EXPERT_FF), d ** -0.5),
        'moe_w3': nrm((DEPTH, N_EXPERTS, d, EXPERT_FF), d ** -0.5),
        'moe_w2': nrm((DEPTH, N_EXPERTS, EXPERT_FF, d), EXPERT_FF ** -0.5),
    }


def reference(x_prompt, x_sample, cache_attn_k, cache_attn_v, state_ret_fwd, state_ret_bwd, c, c_ctx,
              ada_w, ada_b, norm1_g, norm2_g, ev_w_in, ev_w_out, hy_sconv_w, hy_sconv_b,
              hy_ffn_w1, hy_ffn_b1, hy_ffn_w2, hy_ffn_b2, hy_ffn_w3, hy_freq, hy_bias,
              at_q_norm, at_k_norm, at_sink, od_w_in, od_w_out, ret_decay, ret_gn,
              moe_router, moe_w1, moe_w3, moe_w2):
    xp, xl = x_prompt, x_sample
    new_k, new_v, new_sf, new_sb = [], [], [], []
    cond_ctx = c_ctx[None, :]
    for l in range(DEPTH):
        p_sh1, p_sc1, p_g1, p_sh2, p_sc2, p_g2 = adaln(cond_ctx, ada_w[l], ada_b[l])
        l_sh1, l_sc1, l_g1, l_sh2, l_sc2, l_g2 = adaln(c, ada_w[l], ada_b[l])
        hp = rmsnorm(xp, norm1_g[l]) * (1.0 + p_sc1) + p_sh1
        hl = rmsnorm(xl, norm1_g[l]) * (1.0 + l_sc1) + l_sh1
        j = l // 2
        if l % 2 == 0:
            ew = (ev_w_in[j], ev_w_out[j], hy_sconv_w[j], hy_sconv_b[j], hy_ffn_w1[j], hy_ffn_b1[j],
                  hy_ffn_w2[j], hy_ffn_b2[j], hy_ffn_w3[j], hy_freq[j], hy_bias[j],
                  at_q_norm[j], at_k_norm[j], at_sink[j])
            op, kv_ctx = even_mixer(hp, *ew)
            ol, _ = even_mixer(hl, *ew, ctx_k=cache_attn_k[:, j], ctx_v=cache_attn_v[:, j])
            new_k.append(kv_ctx[0])
            new_v.append(kv_ctx[1])
        else:
            zeros = jnp.zeros((xp.shape[0], C_HEADS, C_DK, C_DV), jnp.float32)
            op, sf, sb = odd_mixer(hp, od_w_in[j], od_w_out[j], ret_decay[j], ret_gn[j], zeros, zeros)
            ol, _, _ = odd_mixer(hl, od_w_in[j], od_w_out[j], ret_decay[j], ret_gn[j],
                                 state_ret_fwd[:, j], state_ret_bwd[:, j])
            new_sf.append(sf)
            new_sb.append(sb)
        xp = xp + p_g1 * op
        xl = xl + l_g1 * ol
        hp = rmsnorm(xp, norm2_g[l]) * (1.0 + p_sc2) + p_sh2
        hl = rmsnorm(xl, norm2_g[l]) * (1.0 + l_sc2) + l_sh2
        xp = xp + p_g2 * ec_moe(hp, moe_router[l], moe_w1[l], moe_w3[l], moe_w2[l])
        xl = xl + l_g2 * ec_moe(hl, moe_router[l], moe_w1[l], moe_w3[l], moe_w2[l])
    k_out = jnp.stack(new_k, axis=1)
    v_out = jnp.stack(new_v, axis=1)
    sf_out = jnp.stack(new_sf, axis=1)
    sb_out = jnp.stack(new_sb, axis=1)
    return (xp, xl, k_out, v_out, sf_out, sb_out)
```

```python
import functools
import math

import ml_dtypes
import numpy as np
import jax
import jax.numpy as jnp
from jax import lax
from jax.experimental import pallas as pl
from jax.experimental.pallas import tpu as pltpu

F32 = jnp.float32
BF16 = jnp.bfloat16
I32 = jnp.int32
HIGHEST = lax.Precision.HIGHEST

EPS = 1e-6
NEG_INF = -1e30
GRID_W = 64
WINDOW = 128
ATT_BLOCK = 128
RET_CHUNK = 128
ROPE_BASE = 10000.0
A_BANDS = 16
A_SHORT_DECAY_PCT = 0.3
A_LONG_DECAY_PCT = 1.5
A_DECAY_TARGET = 1e-2
CAPACITY_FACTOR = 2

TOK_TILE = 256
LANE = 128
FFT_N2 = 128
VMEM_LIMIT = 56 * 1024 * 1024


def _pick(n, cands):
    for c in cands:
        if n % c == 0:
            return c
    raise ValueError(f"no tile for {n} in {cands}")


def _call(kernel, *, grid, in_specs, out_specs, out_shape, scratch=(), nsp=0, name=None):
    gs = pltpu.PrefetchScalarGridSpec(num_scalar_prefetch=nsp, grid=grid, in_specs=in_specs,
                                      out_specs=out_specs, scratch_shapes=list(scratch))
    cp = pltpu.CompilerParams(dimension_semantics=("arbitrary",) * len(grid), vmem_limit_bytes=VMEM_LIMIT)
    return pl.pallas_call(kernel, grid_spec=gs, out_shape=out_shape, compiler_params=cp, name=name)


def _silu(x):
    return x / (1.0 + jnp.exp(-x))


def _bdot(a, b):
    return jnp.dot(a, b, preferred_element_type=F32)


def _split(x):
    hi = x.astype(BF16)
    lo = (x - hi.astype(F32)).astype(BF16)
    return hi, lo


def _dot3(mh, ml, x):
    xh, xl = _split(x)
    return _bdot(mh, xh) + _bdot(mh, xl) + _bdot(ml, xh)


def _adaln_kernel(c_ref, w_ref, b_ref, o_ref):
    s = _silu(c_ref[...])
    o_ref[0] = _bdot(s.astype(BF16), w_ref[0].astype(BF16)) + b_ref[0]


def adaln_all(cond, ada_w, ada_b):
    depth, d, n6 = ada_w.shape
    rows = cond.shape[0]
    tn = _pick(n6, (1024, 512, 256, 128))
    return _call(
        _adaln_kernel, grid=(depth, n6 // tn),
        in_specs=[pl.BlockSpec((rows, d), lambda l, j: (0, 0)),
                  pl.BlockSpec((1, d, tn), lambda l, j: (l, 0, j)),
                  pl.BlockSpec((1, 1, tn), lambda l, j: (l, 0, j))],
        out_specs=pl.BlockSpec((1, rows, tn), lambda l, j: (l, 0, j)),
        out_shape=jax.ShapeDtypeStruct((depth, rows, n6), F32), name="adaln",
    )(cond, ada_w, ada_b.reshape(depth, 1, n6))


def _norm_mod(x_ref, g_ref, sc_ref, sh_ref):
    x = x_ref[...]
    ms = jnp.mean(x * x, axis=-1, keepdims=True)
    y = x * lax.rsqrt(ms + EPS) * g_ref[...]
    return y * (1.0 + sc_ref[0, 0]) + sh_ref[0, 0]


def _norm_mod_kernel(cidx_ref, x_ref, g_ref, sc_ref, sh_ref, o_ref):
    o_ref[...] = _norm_mod(x_ref, g_ref, sc_ref, sh_ref).astype(o_ref.dtype)


def _norm_router_kernel(cidx_ref, x_ref, g_ref, sc_ref, sh_ref, rt_ref, h_ref, aff_ref):
    h = _norm_mod(x_ref, g_ref, sc_ref, sh_ref)
    h_ref[...] = h
    logits = lax.dot_general(rt_ref[...], h, (((1,), (1,)), ((), ())), precision=HIGHEST,
                             preferred_element_type=F32)
    m = jnp.max(logits, axis=0, keepdims=True)
    e = jnp.exp(logits - m)
    aff_ref[...] = e / jnp.sum(e, axis=0, keepdims=True)


def norm_mod(x, gain, mods, cidx, k_shift, k_scale, *, router_t=None):
    t, d = x.shape
    tm = TOK_TILE
    specs = [pl.BlockSpec((tm, d), lambda i, c: (i, 0)),
             pl.BlockSpec((1, d), lambda i, c: (0, 0)),
             pl.BlockSpec((1, 1, 1, d), lambda i, c: (c[i], k_scale, 0, 0)),
             pl.BlockSpec((1, 1, 1, d), lambda i, c: (c[i], k_shift, 0, 0))]
    if router_t is None:
        return _call(_norm_mod_kernel, grid=(t // tm,), nsp=1, in_specs=specs,
                     out_specs=pl.BlockSpec((tm, d), lambda i, c: (i, 0)),
                     out_shape=jax.ShapeDtypeStruct((t, d), BF16), name="norm_mod",
                     )(cidx, x, gain.reshape(1, d), mods, mods)
    e = router_t.shape[0]
    specs.append(pl.BlockSpec((e, d), lambda i, c: (0, 0)))
    return _call(_norm_router_kernel, grid=(t // tm,), nsp=1, in_specs=specs,
                 out_specs=[pl.BlockSpec((tm, d), lambda i, c: (i, 0)),
                            pl.BlockSpec((e, tm), lambda i, c: (0, i))],
                 out_shape=[jax.ShapeDtypeStruct((t, d), F32), jax.ShapeDtypeStruct((e, t), F32)],
                 name="norm_router",
                 )(cidx, x, gain.reshape(1, d), mods, mods, router_t)


def _mm_kernel(*refs, nx, gated):
    pos = 1 if gated else 0
    x_refs = refs[pos:pos + nx]
    w_refs = refs[pos + nx:pos + 2 * nx]
    pos += 2 * nx
    if gated:
        res_ref, gate_ref = refs[pos], refs[pos + 1]
        pos += 2
    o_ref = refs[pos]
    wb_refs = refs[pos + 1:]

    @pl.when(pl.program_id(1) == 0)
    def _():
        for w_ref, wb in zip(w_refs, wb_refs):
            wb[...] = w_ref[...].astype(BF16)

    acc = _bdot(x_refs[0][...], wb_refs[0][...])
    for x_ref, wb in zip(x_refs[1:], wb_refs[1:]):
        acc = acc + _bdot(x_ref[...], wb[...])
    if gated:
        acc = res_ref[...] + gate_ref[0, 0] * acc
    o_ref[...] = acc.astype(o_ref.dtype)


def matmul(xs, w, *, out_dtype, tm, tn=512, res=None, mods=None, cidx=None, k_gate=None):
    m = xs[0].shape[0]
    kk = xs[0].shape[1]
    n = w.shape[1]
    nx = len(xs)
    assert all(x.shape == (m, kk) for x in xs) and w.shape[0] == nx * kk
    tn = _pick(n, (tn, 512, 256, 128))
    gated = res is not None
    if gated:
        imap = lambda f: (lambda j, i, c: f(j, i, c))
    else:
        imap = lambda f: (lambda j, i: f(j, i, None))
    specs = [pl.BlockSpec((tm, kk), imap(lambda j, i, c: (i, 0))) for _ in xs]
    specs += [pl.BlockSpec((kk, tn), imap(lambda j, i, c, q=q: (q, j))) for q in range(nx)]
    args = list(xs) + [w] * nx
    if gated:
        d = mods.shape[-1]
        assert d == n
        specs += [pl.BlockSpec((tm, tn), imap(lambda j, i, c: (i, j))),
                  pl.BlockSpec((1, 1, 1, tn), imap(lambda j, i, c: (c[i], k_gate, 0, j)))]
        args = [cidx] + args + [res, mods]
    return _call(functools.partial(_mm_kernel, nx=nx, gated=gated), grid=(n // tn, m // tm),
                 nsp=1 if gated else 0, in_specs=specs,
                 out_specs=pl.BlockSpec((tm, tn), imap(lambda j, i, c: (i, j))),
                 out_shape=jax.ShapeDtypeStruct((m, n), out_dtype),
                 scratch=[pltpu.VMEM((kk, tn), BF16) for _ in xs], name="matmul",
                 )(*args)


def _hy_pre_kernel(first_ref, last_ref, *refs):
    ins = refs[:15]
    z_ref, x0_ref = refs[15], refs[16]
    i = pl.program_id(0)
    is_first = first_ref[i] == 1
    is_last = last_ref[i] == 1

    def sconv(m_ref, p_ref, n_ref, w_ref, b_ref):
        x = m_ref[...]
        tl = x.shape[0]
        rows = lax.broadcasted_iota(I32, x.shape, 0)
        prev_row = jnp.where(is_first, 0.0, p_ref[7:8, :])
        next_row = jnp.where(is_last, 0.0, n_ref[0:1, :])
        xm1 = jnp.where(rows == 0, prev_row, pltpu.roll(x, 1, 0))
        xp1 = jnp.where(rows == tl - 1, next_row, pltpu.roll(x, tl - 1, 0))
        w = w_ref[...]
        return xm1 * w[0:1] + x * w[1:2] + xp1 * w[2:3] + b_ref[...]

    hv = sconv(ins[0], ins[1], ins[2], ins[9], ins[12])
    x0 = sconv(ins[3], ins[4], ins[5], ins[10], ins[13])
    x1 = sconv(ins[6], ins[7], ins[8], ins[11], ins[14])
    z_ref[...] = x1 * hv
    x0_ref[...] = x0


def hy_pre(u, sc_w, sc_b, first, last, a_width):
    t = u.shape[0]
    tl = TOK_TILE
    cb = _pick(a_width, (512, 256, 128))
    ncb = a_width // cb
    nt8 = t // 8
    specs = []
    for part in range(3):
        off = part * ncb
        specs += [pl.BlockSpec((tl, cb), lambda i, j, f, l, off=off: (i, off + j)),
                  pl.BlockSpec((8, cb), lambda i, j, f, l, off=off: (jnp.maximum(i * (tl // 8) - 1, 0), off + j)),
                  pl.BlockSpec((8, cb), lambda i, j, f, l, off=off: (jnp.minimum((i + 1) * (tl // 8), nt8 - 1), off + j))]
    for part in range(3):
        off = part * ncb
        specs.append(pl.BlockSpec((3, cb), lambda i, j, f, l, off=off: (0, off + j)))
    for part in range(3):
        off = part * ncb
        specs.append(pl.BlockSpec((1, cb), lambda i, j, f, l, off=off: (0, off + j)))
    out_spec = pl.BlockSpec((tl, cb), lambda i, j, f, l: (i, j))
    scb = sc_b.reshape(1, -1)
    return _call(_hy_pre_kernel, grid=(t // tl, ncb), nsp=2, in_specs=specs,
                 out_specs=[out_spec, out_spec],
                 out_shape=[jax.ShapeDtypeStruct((t, a_width), F32)] * 2, name="hy_pre",
                 )(first, last, *([u] * 9), sc_w, sc_w, sc_w, scb, scb, scb)


def _filter_kernel(feats_ref, w1_ref, b1_ref, w2_ref, b2_ref, fr_ref, w3f_ref, w3b_ref, dl_ref, hf_ref, hb_ref):
    feats = feats_ref[...]
    hdot = functools.partial(jnp.dot, precision=HIGHEST, preferred_element_type=F32)
    h = jnp.sin(fr_ref[0:1, :] * (hdot(feats, w1_ref[...]) + b1_ref[...]))
    h = jnp.sin(fr_ref[1:2, :] * (hdot(h, w2_ref[...]) + b2_ref[...]))
    dec = jnp.exp(-feats[:, 0:1] * dl_ref[...])
    hf = hdot(h, w3f_ref[...]) * dec
    hb = hdot(h, w3b_ref[...]) * dec
    s = (jnp.sum(jnp.abs(hf), axis=0, keepdims=True) + jnp.sum(jnp.abs(hb), axis=0, keepdims=True)) + EPS
    hf_ref[...] = hf / s
    hb_ref[...] = hb / s


def hyena_filters(seq, w1, b1, w2, b2, w3, freq, a_width):
    t = jnp.arange(seq, dtype=F32) / seq
    bands = jnp.linspace(1e-4, A_BANDS - 1, A_BANDS, dtype=F32)
    ang = 2.0 * math.pi * t[:, None] * bands[None, :]
    feats = jnp.concatenate([t[:, None], jnp.cos(ang), -jnp.sin(ang)], axis=-1)
    emb = feats.shape[1]
    feats = jnp.pad(feats, ((0, 0), (0, LANE - emb)))
    w1p = jnp.pad(w1, ((0, LANE - emb), (0, 0)))
    max_decay = math.log(A_DECAY_TARGET) / A_SHORT_DECAY_PCT
    min_decay = math.log(A_DECAY_TARGET) / A_LONG_DECAY_PCT
    deltas = jnp.abs(jnp.linspace(min_decay, max_decay, a_width, dtype=F32)).reshape(1, a_width)
    ffn = w2.shape[0]
    cb = _pick(a_width, (256, 128))
    ncb = a_width // cb
    full = lambda shape: pl.BlockSpec(shape, lambda j: (0,) * len(shape))
    out_spec = pl.BlockSpec((seq, cb), lambda j: (0, j))
    return _call(_filter_kernel, grid=(ncb,),
                 in_specs=[full((seq, LANE)), full((LANE, ffn)), full((1, ffn)), full((ffn, ffn)), full((1, ffn)),
                           full((2, ffn)),
                           pl.BlockSpec((ffn, cb), lambda j: (0, j)),
                           pl.BlockSpec((ffn, cb), lambda j: (0, ncb + j)),
                           pl.BlockSpec((1, cb), lambda j: (0, j))],
                 out_specs=[out_spec, out_spec],
                 out_shape=[jax.ShapeDtypeStruct((seq, a_width), F32)] * 2, name="hy_filter",
                 )(feats, w1p, b1.reshape(1, ffn), w2, b2.reshape(1, ffn), freq, w3, w3, deltas)


def _hilo(x):
    x32 = np.asarray(x, np.float32)
    hi = x32.astype(ml_dtypes.bfloat16)
    lo = (x32 - hi.astype(np.float32)).astype(ml_dtypes.bfloat16)
    return hi, lo


@functools.lru_cache(maxsize=None)
def _dft_direct_consts(seq):
    n = 2 * seq
    k = np.arange(n)
    ang = 2.0 * np.pi * ((k[:, None] * k[None, :]) % n) / n
    c, s = np.cos(ang), np.sin(ang)
    fwd = np.block([[c[:, :seq], s[:, :seq]], [-s[:, :seq], c[:, :seq]]])
    taps = np.concatenate([c, -s], axis=0)
    inv = np.block([[c[:seq, :], -s[:seq, :]], [s[:seq, :], c[:seq, :]]]) / n
    return _hilo(fwd), _hilo(taps), _hilo(inv)


@functools.lru_cache(maxsize=None)
def _dft_two_stage_consts(seq):
    n = 2 * seq
    n2c = FFT_N2
    n1c = n // n2c
    h = n1c // 2
    n2 = np.arange(n2c)[:, None, None]
    k1 = np.arange(n1c)[None, :, None]
    n1 = np.arange(n1c)[None, None, :]
    psi = 2.0 * np.pi * ((n1 * k1 * n2c + n2 * k1) % n) / n
    c, s = np.cos(psi), np.sin(psi)
    m1d = np.concatenate([np.concatenate([c[:, :, :h], s[:, :, :h]], axis=2),
                          np.concatenate([-s[:, :, :h], c[:, :, :h]], axis=2)], axis=1)
    m1t = np.concatenate([c, -s], axis=1)
    ct = np.swapaxes(c, 1, 2)[:, :h, :]
    st = np.swapaxes(s, 1, 2)[:, :h, :]
    m1i = np.concatenate([np.concatenate([ct, -st], axis=2),
                          np.concatenate([st, ct], axis=2)], axis=1) / n
    kk = np.arange(n2c)
    th = 2.0 * np.pi * ((kk[:, None] * kk[None, :]) % n2c) / n2c
    c2, s2 = np.cos(th), np.sin(th)
    m2 = np.block([[c2, s2], [-s2, c2]])
    m2i = np.block([[c2, -s2], [s2, c2]])
    return _hilo(m1d), _hilo(m1t), _hilo(m1i), _hilo(m2), _hilo(m2i)


def _cmul_stacked(x, hs, half):
    xr, xi = x[:half], x[half:]
    hr, hi = hs[:half], hs[half:]
    return jnp.concatenate([xr * hr - xi * hi, xr * hi + xi * hr], axis=0)


def _const_mm3_kernel(mh_ref, ml_ref, x_ref, o_ref):
    o_ref[...] = _dot3(mh_ref[...], ml_ref[...], x_ref[...])


def const_mm3(mh, ml, x):
    m, k = mh.shape
    c = x.shape[1]
    cb = _pick(c, (256, 128))
    return _call(_const_mm3_kernel, grid=(c // cb,),
                 in_specs=[pl.BlockSpec((m, k), lambda j: (0, 0)), pl.BlockSpec((m, k), lambda j: (0, 0)),
                           pl.BlockSpec((k, cb), lambda j: (0, j))],
                 out_specs=pl.BlockSpec((m, cb), lambda j: (0, j)),
                 out_shape=jax.ShapeDtypeStruct((m, c), F32), name="dft_taps")(mh, ml, x)


def _conv_direct_kernel(fh_ref, fl_ref, gh_ref, gl_ref, zr_ref, zi_ref, h_ref, yr_ref, yi_ref, *, seq):
    x = jnp.concatenate([zr_ref[...], zi_ref[...]], axis=0)
    spec = _dot3(fh_ref[...], fl_ref[...], x)
    y = _dot3(gh_ref[...], gl_ref[...], _cmul_stacked(spec, h_ref[...], 2 * seq))
    yr_ref[...] = y[:seq]
    yi_ref[...] = y[seq:]


def long_conv_direct(z, row0, nb, seq, hf, hb):
    a_width = z.shape[1]
    (fh, fl), (th, tl_), (gh, gl) = _dft_direct_consts(seq)
    taps = jnp.concatenate([hf, jnp.zeros((1, a_width), F32), hb[:0:-1]], axis=0)
    spec_h = const_mm3(jnp.asarray(th), jnp.asarray(tl_), taps)
    cb = _pick(a_width, (256, 128))
    half = nb // 2
    b0 = row0 // seq
    full = lambda a: pl.BlockSpec(a.shape, lambda j, p: (0, 0))
    yr, yi = _call(
        functools.partial(_conv_direct_kernel, seq=seq), grid=(a_width // cb, half),
        in_specs=[full(fh), full(fl), full(gh), full(gl),
                  pl.BlockSpec((seq, cb), lambda j, p: (b0 + p, j)),
                  pl.BlockSpec((seq, cb), lambda j, p: (b0 + half + p, j)),
                  pl.BlockSpec((4 * seq, cb), lambda j, p: (0, j))],
        out_specs=[pl.BlockSpec((seq, cb), lambda j, p: (p, j))] * 2,
        out_shape=[jax.ShapeDtypeStruct((half * seq, a_width), F32)] * 2, name="conv_direct",
    )(jnp.asarray(fh), jnp.asarray(fl), jnp.asarray(gh), jnp.asarray(gl), z, z, spec_h)
    return jnp.concatenate([yr, yi], axis=0)


def _fft_s1_kernel(mh_ref, ml_ref, *refs, g, packed):
    o_ref = refs[-1]
    for j in range(g):
        if packed:
            x = jnp.concatenate([refs[0][0, j], refs[1][0, j]], axis=0)
        else:
            x = refs[0][0, j]
        o_ref[0, j] = _dot3(mh_ref[j], ml_ref[j], x)


def _fft_s1(mh, ml, zp, *, packed):
    n2c, m_rows, n1c = mh.shape
    bsz, _, rows, c = zp.shape
    p = bsz // 2 if packed else bsz
    cb = _pick(c, (256, 128))
    g = 16
    cspec = pl.BlockSpec((g, m_rows, n1c), lambda q, j, t: (t, 0, 0))
    zspecs = [pl.BlockSpec((1, g, rows, cb), lambda q, j, t: (q, t, 0, j))]
    args = [zp]
    if packed:
        zspecs.append(pl.BlockSpec((1, g, rows, cb), lambda q, j, t: (q + p, t, 0, j)))
        args.append(zp)
    return _call(functools.partial(_fft_s1_kernel, g=g, packed=packed), grid=(p, c // cb, n2c // g),
                 in_specs=[cspec, cspec] + zspecs,
                 out_specs=pl.BlockSpec((1, g, m_rows, cb), lambda q, j, t: (q, t, 0, j)),
                 out_shape=jax.ShapeDtypeStruct((p, n2c, m_rows, c), F32), name="fft_s1",
                 )(jnp.asarray(mh), jnp.asarray(ml), *args)


def _fft_s2_kernel(mh_ref, ml_ref, ih_ref, il_ref, *refs, g, conv):
    a_ref, o_ref = refs[0], refs[-1]
    half = a_ref.shape[2] // 2
    for j in range(g):
        spec = _dot3(mh_ref[...], ml_ref[...], a_ref[0, j])
        if conv:
            spec = _dot3(ih_ref[...], il_ref[...], _cmul_stacked(spec, refs[1][0, j], half))
        o_ref[0, j] = spec


def _fft_s2(m2, m2i, at, spec_h=None):
    p, n1c, rows, c = at.shape
    cb = _pick(c, (256, 128))
    g = _pick(n1c, (8, 4, 2, 1))
    conv = spec_h is not None
    mspec = pl.BlockSpec((rows, rows), lambda q, j, t: (0, 0))
    blk = pl.BlockSpec((1, g, rows, cb), lambda q, j, t: (q, t, 0, j))
    specs = [mspec] * 4 + [blk]
    args = [jnp.asarray(m2[0]), jnp.asarray(m2[1]), jnp.asarray(m2i[0]), jnp.asarray(m2i[1]), at]
    if conv:
        specs.append(pl.BlockSpec((1, g, rows, cb), lambda q, j, t: (0, t, 0, j)))
        args.append(spec_h)
    return _call(functools.partial(_fft_s2_kernel, g=g, conv=conv), grid=(p, c // cb, n1c // g),
                 in_specs=specs, out_specs=blk, out_shape=jax.ShapeDtypeStruct(at.shape, F32), name="fft_s2",
                 )(*args)


def _fft_s3_kernel(mh_ref, ml_ref, b_ref, yr_ref, yi_ref, *, g):
    half = yr_ref.shape[2]
    for j in range(g):
        y = _dot3(mh_ref[j], ml_ref[j], b_ref[0, j])
        yr_ref[0, j] = y[:half]
        yi_ref[0, j] = y[half:]


def _fft_s3(mh, ml, bt):
    n2c, n1c, rows = mh.shape
    p, _, _, c = bt.shape
    cb = _pick(c, (256, 128))
    g = 16
    h = n1c // 2
    oblk = pl.BlockSpec((1, g, h, cb), lambda q, j, t: (q, t, 0, j))
    cspec = pl.BlockSpec((g, n1c, rows), lambda q, j, t: (t, 0, 0))
    return _call(functools.partial(_fft_s3_kernel, g=g), grid=(p, c // cb, n2c // g),
                 in_specs=[cspec, cspec, pl.BlockSpec((1, g, rows, cb), lambda q, j, t: (q, t, 0, j))],
                 out_specs=[oblk, oblk],
                 out_shape=[jax.ShapeDtypeStruct((p, n2c, h, c), F32)] * 2, name="fft_s3",
                 )(jnp.asarray(mh), jnp.asarray(ml), bt)


def _swap_digits(a):
    p, x, y2, c = a.shape
    y = y2 // 2
    return a.reshape(p, x, 2, y, c).transpose(0, 3, 2, 1, 4).reshape(p, y, 2 * x, c)


def long_conv_two_stage(z, row0, nb, seq, hf, hb):
    a_width = z.shape[1]
    m1d, m1t, m1i, m2, m2i = _dft_two_stage_consts(seq)
    n2c = FFT_N2
    n1c = 2 * seq // n2c
    taps = jnp.concatenate([hf, jnp.zeros((1, a_width), F32), hb[:0:-1]], axis=0)
    taps_p = taps.reshape(1, n1c, n2c, a_width).transpose(0, 2, 1, 3)
    spec_h = _fft_s2(m2, m2i, _swap_digits(_fft_s1(m1t[0], m1t[1], taps_p, packed=False)))
    zs = lax.slice_in_dim(z, row0, row0 + nb * seq, axis=0)
    zp = zs.reshape(nb, n1c // 2, n2c, a_width).transpose(0, 2, 1, 3)
    a = _swap_digits(_fft_s1(m1d[0], m1d[1], zp, packed=True))
    b = _swap_digits(_fft_s2(m2, m2i, a, spec_h))
    yr, yi = _fft_s3(m1i[0], m1i[1], b)
    y = jnp.concatenate([yr, yi], axis=0)
    return y.transpose(0, 2, 1, 3).reshape(nb * seq, a_width)


def _hy_post_kernel(x0_ref, z_ref, bias_ref, cp_ref, cs_ref, o_ref, *, ntp):
    conv = jnp.where(pl.program_id(0) < ntp, cp_ref[...], cs_ref[...])
    o_ref[...] = (x0_ref[...] * (conv + z_ref[...] * bias_ref[...])).astype(o_ref.dtype)


def hy_post(x0, z, bias, conv_p, conv_s):
    t, a_width = x0.shape
    tl = TOK_TILE
    ntp = conv_p.shape[0] // tl
    nts = conv_s.shape[0] // tl
    cb = _pick(a_width, (512, 256, 128))
    blk = pl.BlockSpec((tl, cb), lambda i, j: (i, j))
    return _call(functools.partial(_hy_post_kernel, ntp=ntp), grid=(t // tl, a_width // cb),
                 in_specs=[blk, blk, pl.BlockSpec((1, cb), lambda i, j: (0, j)),
                           pl.BlockSpec((tl, cb), lambda i, j: (jnp.minimum(i, ntp - 1), j)),
                           pl.BlockSpec((tl, cb), lambda i, j: (jnp.clip(i - ntp, 0, nts - 1), j))],
                 out_specs=blk, out_shape=jax.ShapeDtypeStruct((t, a_width), BF16), name="hy_post",
                 )(x0, z, bias.reshape(1, a_width), conv_p, conv_s)


def _qkv_prep_kernel(q_ref, k_ref, v_ref, qn_ref, kn_ref, cos_ref, sin_ref, qo_ref, ko_ref, vo_ref, kc_ref, *, hd, scale):
    cos = cos_ref[...]
    sin = sin_ref[...]
    lane = lax.broadcasted_iota(I32, cos.shape, 1)
    low = (lane % (hd // 2)) < (hd // 4)

    def norm(x, g):
        return x * lax.rsqrt(jnp.mean(x * x, axis=-1, keepdims=True) + EPS) * g

    def rope(x):
        partner = jnp.where(low, pltpu.roll(x, hd - hd // 4, 1), pltpu.roll(x, hd // 4, 1))
        return x * cos + partner * sin

    for h in range(q_ref.shape[1] // hd):
        sl = slice(h * hd, (h + 1) * hd)
        qo_ref[:, sl] = (rope(norm(q_ref[:, sl], qn_ref[...])) * scale).astype(qo_ref.dtype)
    for h in range(k_ref.shape[1] // hd):
        sl = slice(h * hd, (h + 1) * hd)
        kn = norm(k_ref[:, sl], kn_ref[...])
        kc_ref[:, sl] = kn
        ko_ref[:, sl] = rope(kn).astype(ko_ref.dtype)
    vo_ref[...] = v_ref[...].astype(vo_ref.dtype)


def qkv_prep(u, col0, n_heads, n_kv, hd, qn, kn, cos_t, sin_t):
    t = u.shape[0]
    tl = TOK_TILE
    qw, kw = n_heads * hd, n_kv * hd
    assert col0 % qw == 0 and (col0 + qw) % kw == 0
    qb, kb = col0 // qw, (col0 + qw) // kw
    row = lambda w: pl.BlockSpec((tl, w), lambda i: (i, 0))
    vec = pl.BlockSpec((1, hd), lambda i: (0, 0))
    return _call(functools.partial(_qkv_prep_kernel, hd=hd, scale=hd ** -0.5), grid=(t // tl,),
                 in_specs=[pl.BlockSpec((tl, qw), lambda i: (i, qb)),
                           pl.BlockSpec((tl, kw), lambda i: (i, kb)),
                           pl.BlockSpec((tl, kw), lambda i: (i, kb + 1)),
                           vec, vec, row(hd), row(hd)],
                 out_specs=[row(qw), row(kw), row(kw), row(kw)],
                 out_shape=[jax.ShapeDtypeStruct((t, qw), BF16), jax.ShapeDtypeStruct((t, kw), BF16),
                            jax.ShapeDtypeStruct((t, kw), BF16), jax.ShapeDtypeStruct((t, kw), F32)],
                 name="qkv_prep")(u, u, u, qn.reshape(1, hd), kn.reshape(1, hd), cos_t, sin_t)


def _rope_tables(seq, hd):
    rows = seq // GRID_W
    row = jnp.repeat(jnp.arange(rows), GRID_W).astype(F32)
    col = jnp.tile(jnp.arange(GRID_W), rows).astype(F32)
    nf = hd // 4
    inv = jnp.exp(-math.log(ROPE_BASE) * jnp.arange(nf, dtype=F32) / nf)
    ang_r = row[:, None] * inv[None, :]
    ang_c = col[:, None] * inv[None, :]
    cos = jnp.concatenate([jnp.cos(ang_r)] * 2 + [jnp.cos(ang_c)] * 2, axis=-1)
    sin = jnp.concatenate([-jnp.sin(ang_r), jnp.sin(ang_r), -jnp.sin(ang_c), jnp.sin(ang_c)], axis=-1)
    return cos, sin


def _attn_kernel(sink_ref, q_ref, *refs, kinds, group, hd, nblk):
    nk = len(kinds)
    k_refs, v_refs, o_ref = refs[:nk], refs[nk:2 * nk], refs[2 * nk]
    kvg = pl.program_id(1)
    i = pl.program_id(2)
    tq = q_ref.shape[0]
    q = jnp.concatenate([q_ref[:, g * hd:(g + 1) * hd] for g in range(group)], axis=0)
    rows = lax.broadcasted_iota(I32, (group * tq, 1), 0)
    sink = jnp.full((group * tq, 1), sink_ref[kvg * group], F32)
    for g in range(1, group):
        sink = jnp.where(rows >= g * tq, sink_ref[kvg * group + g], sink)

    def load(ref):
        x = ref[0] if len(ref.shape) == 3 else ref[...]
        return x.astype(BF16)

    scores = []
    m = sink
    for kind, k_ref in zip(kinds, k_refs):
        s = lax.dot_general(q, load(k_ref), (((1,), (1,)), ((), ())), preferred_element_type=F32)
        if kind in ("prev", "next"):
            r = lax.broadcasted_iota(I32, s.shape, 0) % tq
            c = lax.broadcasted_iota(I32, s.shape, 1)
            if kind == "prev":
                ok = c >= r + jnp.where(i > 0, 0, 2 * tq)
            else:
                ok = c <= r - jnp.where(i < nblk - 1, 0, 2 * tq)
            s = jnp.where(ok, s, NEG_INF)
        scores.append(s)
        m = jnp.maximum(m, jnp.max(s, axis=-1, keepdims=True))
    den = jnp.exp(sink - m)
    acc = None
    for s, v_ref in zip(scores, v_refs):
        p = jnp.exp(s - m)
        den = den + jnp.sum(p, axis=-1, keepdims=True)
        pv = _bdot(p.astype(BF16), load(v_ref))
        acc = pv if acc is None else acc + pv
    out = acc / den
    o_ref[...] = jnp.concatenate([out[g * tq:(g + 1) * tq] for g in range(group)], axis=1).astype(o_ref.dtype)


def attn_context(q, k, v, sink, row0, nb, seq, n_kv, hd):
    group = q.shape[1] // (n_kv * hd)
    b0 = row0 // seq
    kv = pl.BlockSpec((seq, hd), lambda b, g, i, s: (b0 + b, g))
    return _call(functools.partial(_attn_kernel, kinds=("all",), group=group, hd=hd, nblk=1),
                 grid=(nb, n_kv, 1), nsp=1,
                 in_specs=[pl.BlockSpec((seq, group * hd), lambda b, g, i, s: (b0 + b, g)), kv, kv],
                 out_specs=pl.BlockSpec((seq, group * hd), lambda b, g, i, s: (b, g)),
                 out_shape=jax.ShapeDtypeStruct((nb * seq, q.shape[1]), BF16), name="attn_context",
                 )(sink, q, k, v)


def attn_latent(q, k, v, ck, cv, sink, row0, nb, seq, n_kv, hd):
    group = q.shape[1] // (n_kv * hd)
    blk = ATT_BLOCK
    nblk = seq // blk
    r0 = row0 // blk
    past = ck.shape[1]
    cur = lambda b, g, i, s: (r0 + b * nblk + i, g)
    prev = lambda b, g, i, s: (r0 + b * nblk + jnp.maximum(i - 1, 0), g)
    nxt = lambda b, g, i, s: (r0 + b * nblk + jnp.minimum(i + 1, nblk - 1), g)
    ctx = pl.BlockSpec((1, past, hd), lambda b, g, i, s: (b, 0, g))
    kvs = [pl.BlockSpec((blk, hd), f) for f in (prev, cur, nxt)]
    return _call(functools.partial(_attn_kernel, kinds=("prev", "cur", "next", "ctx"), group=group, hd=hd, nblk=nblk),
                 grid=(nb, n_kv, nblk), nsp=1,
                 in_specs=[pl.BlockSpec((blk, group * hd), cur)] + kvs + [ctx] + kvs + [ctx],
                 out_specs=pl.BlockSpec((blk, group * hd), lambda b, g, i, s: (b * nblk + i, g)),
                 out_shape=jax.ShapeDtypeStruct((nb * seq, q.shape[1]), BF16), name="attn_latent",
                 )(sink, q, k, k, k, ck, v, v, v, cv)


def _ret_kernel(lg_ref, q_ref, k_ref, v_ref, g_ref, gn_ref, *refs, nc, dk, has_init, emit_state):
    pos = 0
    if has_init:
        s0_refs = refs[0:2]
        pos = 2
    y_ref = refs[pos]
    pos += 1
    if emit_state:
        so_refs = refs[pos:pos + 2]
        pos += 2
    o_scr, s_scr = refs[pos], refs[pos + 1]
    h = pl.program_id(1)
    ch = RET_CHUNK
    kscale = dk ** -0.5
    ii = lax.broadcasted_iota(I32, (ch, ch), 0).astype(F32)
    jj = lax.broadcasted_iota(I32, (ch, ch), 1).astype(F32)
    idx = lax.broadcasted_iota(I32, (ch, 1), 0).astype(F32)
    gn = gn_ref[0]

    for d in range(2):
        lg = lg_ref[d, h]
        diff = ii - jj if d == 0 else jj - ii
        dmat = jnp.where(diff >= 0, jnp.exp(lg * jnp.maximum(diff, 0.0)), 0.0) * kscale
        xi = jnp.exp(lg * (idx + 1.0)) if d == 0 else jnp.exp(lg * (ch - idx))
        zeta = (jnp.exp(lg * (ch - 1.0 - idx)) if d == 0 else jnp.exp(lg * idx)) * kscale
        cdec = jnp.exp(lg * jnp.full((1, 1), float(ch), F32))
        if has_init:
            s_scr[...] = s0_refs[d][0, 0, 0]
        else:
            s_scr[...] = jnp.zeros_like(s_scr)

        def body(n, carry, d=d, dmat=dmat, xi=xi, zeta=zeta, cdec=cdec):
            c = n if d == 0 else nc - 1 - n
            r0 = pl.multiple_of(c * ch, ch)
            qc = q_ref[0, pl.ds(r0, ch), :]
            kc = k_ref[0, pl.ds(r0, ch), :]
            vc = v_ref[0, pl.ds(r0, ch), :]
            sc = lax.dot_general(qc, kc, (((1,), (1,)), ((), ())), preferred_element_type=F32) * dmat
            s = s_scr[...]
            o = _bdot(sc.astype(BF16), vc) + _bdot(qc, s.astype(BF16)) * xi
            kz = (kc.astype(F32) * zeta).astype(BF16)
            s_scr[...] = s * cdec + lax.dot_general(kz, vc, (((0,), (0,)), ((), ())), preferred_element_type=F32)
            if d == 0:
                o_scr[pl.ds(r0, ch), :] = o
            else:
                o = o_scr[pl.ds(r0, ch), :] + o
                mu = jnp.mean(o, axis=-1, keepdims=True)
                var = jnp.mean(jnp.square(o - mu), axis=-1, keepdims=True)
                on = (o - mu) * lax.rsqrt(var + EPS) * gn
                gate = _silu(g_ref[0, pl.ds(r0, ch), :].astype(F32))
                y_ref[0, pl.ds(r0, ch), :] = (gate * on).astype(y_ref.dtype)
            return carry

        lax.fori_loop(0, nc, body, 0)
        if emit_state:
            so_refs[d][0, 0, 0] = s_scr[...]


def retention(qkvg, row0, nb, seq, n_heads, dk, dv, log_gamma, gn, *, s_f0=None, s_b0=None, emit_state=False):
    x3 = qkvg.reshape(qkvg.shape[0] // seq, seq, qkvg.shape[1])
    b0 = row0 // seq
    has_init = s_f0 is not None
    ratio = dv // dk
    specs = [pl.BlockSpec(memory_space=pltpu.SMEM),
             pl.BlockSpec((1, seq, dk), lambda b, h: (b0 + b, 0, h)),
             pl.BlockSpec((1, seq, dk), lambda b, h: (b0 + b, 0, n_heads + h)),
             pl.BlockSpec((1, seq, dv), lambda b, h: (b0 + b, 0, 2 * n_heads // ratio + h)),
             pl.BlockSpec((1, seq, dv), lambda b, h: (b0 + b, 0, 2 * n_heads // ratio + n_heads + h)),
             pl.BlockSpec((1, 1, dv), lambda b, h: (h, 0, 0))]
    args = [log_gamma, x3, x3, x3, x3, gn.reshape(n_heads, 1, dv)]
    st = pl.BlockSpec((1, 1, 1, dk, dv), lambda b, h: (b, 0, h, 0, 0))
    if has_init:
        specs += [st, st]
        args += [s_f0, s_b0]
    out_specs = [pl.BlockSpec((1, seq, dv), lambda b, h: (b, 0, h))]
    out_shape = [jax.ShapeDtypeStruct((nb, seq, n_heads * dv), BF16)]
    if emit_state:
        out_specs += [st, st]
        out_shape += [jax.ShapeDtypeStruct((nb, 1, n_heads, dk, dv), F32)] * 2
    gs = pl.GridSpec(grid=(nb, n_heads), in_specs=specs, out_specs=out_specs,
                     scratch_shapes=[pltpu.VMEM((seq, dv), F32), pltpu.VMEM((dk, dv), F32)])
    outs = pl.pallas_call(
        functools.partial(_ret_kernel, nc=seq // RET_CHUNK, dk=dk, has_init=has_init, emit_state=emit_state),
        grid_spec=gs, out_shape=out_shape, name="retention",
        compiler_params=pltpu.CompilerParams(dimension_semantics=("arbitrary", "arbitrary"),
                                             vmem_limit_bytes=VMEM_LIMIT))(*args)
    y = outs[0].reshape(nb * seq, n_heads * dv)
    return (y,) + tuple(outs[1:])


def _ffn_up_kernel(idx_ref, h_hbm, w1_ref, w3_ref, o_ref, xf_scr, xb_scr, sem, *, tm):
    tile = pl.program_id(0)

    @pl.when(pl.program_id(1) == 0)
    def _():
        base = tile * tm

        def issue(r, carry):
            row = idx_ref[base + r]
            pltpu.make_async_copy(h_hbm.at[pl.ds(row, 1)], xf_scr.at[pl.ds(r, 1)], sem).start()
            return carry

        lax.fori_loop(0, tm, issue, 0)
        pltpu.make_async_copy(h_hbm.at[pl.ds(0, tm)], xf_scr, sem).wait()
        xb_scr[...] = xf_scr[...].astype(BF16)

    x = xb_scr[...]
    a = _bdot(x, w1_ref[0].astype(BF16))
    b = _bdot(x, w3_ref[0].astype(BF16))
    o_ref[...] = (_silu(a) * b).astype(o_ref.dtype)


def ffn_up(h, rows, w1, w3, tm, tiles_per_expert):
    n_exp, d, ff = w1.shape
    nslots = rows.shape[0]
    tf = _pick(ff, (512, 256, 128))
    tpe = tiles_per_expert
    wspec = pl.BlockSpec((1, d, tf), lambda t, f, idx: (t // tpe, 0, f))
    return _call(functools.partial(_ffn_up_kernel, tm=tm), grid=(nslots // tm, ff // tf), nsp=1,
                 in_specs=[pl.BlockSpec(memory_space=pl.ANY), wspec, wspec],
                 out_specs=pl.BlockSpec((tm, tf), lambda t, f, idx: (t, f)),
                 out_shape=jax.ShapeDtypeStruct((nslots, ff), BF16),
                 scratch=[pltpu.VMEM((tm, d), F32), pltpu.VMEM((tm, d), BF16), pltpu.SemaphoreType.DMA(())],
                 name="ffn_up")(rows, h, w1, w3)


def _ffn_down_kernel(x_ref, w_ref, g_ref, o_ref):
    acc = _bdot(x_ref[...], w_ref[0].astype(BF16))
    g = g_ref[...]
    for c in range(acc.shape[1] // LANE):
        o_ref[:, c * LANE:(c + 1) * LANE] = acc[:, c * LANE:(c + 1) * LANE] * g


def ffn_down(hm, w2, gates, tm, tiles_per_expert):
    n_exp, ff, d = w2.shape
    nslots = hm.shape[0]
    tn = _pick(d, (512, 256, 128))
    tpe = tiles_per_expert
    return _call(_ffn_down_kernel, grid=(nslots // tm, d // tn),
                 in_specs=[pl.BlockSpec((tm, ff), lambda t, j: (t, 0)),
                           pl.BlockSpec((1, ff, tn), lambda t, j: (t // tpe, 0, j)),
                           pl.BlockSpec((tm, LANE), lambda t, j: (t, 0))],
                 out_specs=pl.BlockSpec((tm, tn), lambda t, j: (t, j)),
                 out_shape=jax.ShapeDtypeStruct((nslots, d), F32), name="ffn_down")(hm, w2, gates)


COMBINE_BATCH = 8


def _combine_kernel(src_ref, c0_ref, nch_ref, cidx_ref, y_hbm, tok_ref, x_ref, g_ref, o_ref, z_scr, acc_scr, sem):
    i = pl.program_id(0)
    tl = x_ref.shape[0]
    acc_scr[...] = jnp.zeros_like(acc_scr)
    t_ids = i * tl + lax.broadcasted_iota(I32, (tl, LANE), 0)
    c0 = c0_ref[i]
    nch = nch_ref[i]
    nbatch = (nch + COMBINE_BATCH - 1) // COMBINE_BATCH

    def batch(bi, carry):
        cb0 = c0 + bi * COMBINE_BATCH
        ncb = jnp.minimum(nch - bi * COMBINE_BATCH, COMBINE_BATCH)

        def issue(r, c2):
            row = src_ref[cb0 * LANE + r]
            pltpu.make_async_copy(y_hbm.at[pl.ds(row, 1)], z_scr.at[pl.ds(r, 1)], sem).start()
            return c2

        lax.fori_loop(0, ncb * LANE, issue, 0)

        def drain(r, c2):
            pltpu.make_async_copy(y_hbm.at[pl.ds(0, LANE)], z_scr.at[pl.ds(0, LANE)], sem).wait()
            return c2

        lax.fori_loop(0, ncb, drain, 0)

        def chunk(k, c2):
            r0 = pl.multiple_of(k * LANE, LANE)
            onehot = jnp.where(t_ids == tok_ref[cb0 + k], 1.0, 0.0).astype(BF16)
            zh, zl = _split(z_scr[pl.ds(r0, LANE), :])
            acc_scr[...] += _bdot(onehot, zh) + _bdot(onehot, zl)
            return c2

        lax.fori_loop(0, ncb, chunk, 0)
        return carry

    lax.fori_loop(0, nbatch, batch, 0)
    o_ref[...] = x_ref[...] + g_ref[0, 0] * acc_scr[...]


def moe_combine(x, y, src, tok_chunks, c0, nch, mods, cidx, k_gate):
    t, d = x.shape
    tl = TOK_TILE
    nchunks = tok_chunks.shape[0]
    return _call(_combine_kernel, grid=(t // tl,), nsp=4,
                 in_specs=[pl.BlockSpec(memory_space=pl.ANY),
                           pl.BlockSpec((nchunks, 1, LANE), lambda i, *_: (0, 0, 0)),
                           pl.BlockSpec((tl, d), lambda i, *_: (i, 0)),
                           pl.BlockSpec((1, 1, 1, d), lambda i, s, a, b, c: (c[i], k_gate, 0, 0))],
                 out_specs=pl.BlockSpec((tl, d), lambda i, *_: (i, 0)),
                 out_shape=jax.ShapeDtypeStruct((t, d), F32),
                 scratch=[pltpu.VMEM((COMBINE_BATCH * LANE, d), F32), pltpu.VMEM((tl, d), F32),
                          pltpu.SemaphoreType.DMA(())],
                 name="moe_combine")(src, c0, nch, cidx, y, tok_chunks, x, mods)


def _route(aff_t, groups):
    n_exp = aff_t.shape[0]
    rows, gates = [], []
    for row0, nb, seq in groups:
        cap = CAPACITY_FACTOR * seq // n_exp
        a = lax.slice_in_dim(aff_t, row0, row0 + nb * seq, axis=1).reshape(n_exp, nb, seq)
        gate, idx = lax.top_k(a, cap)
        tok = row0 + jnp.arange(nb, dtype=I32)[None, :, None] * seq + idx.astype(I32)
        rows.append(tok.reshape(n_exp, nb * cap))
        gates.append(gate.reshape(n_exp, nb * cap))
    return jnp.concatenate(rows, axis=1), jnp.concatenate(gates, axis=1)


def ec_moe(x, gain, mods, cidx, router, w1, w3, w2, groups):
    t, d = x.shape
    n_exp = router.shape[1]
    h, aff_t = norm_mod(x, gain, mods, cidx, 3, 4, router_t=router.T)
    rows, gates = _route(aff_t, groups)
    slots = rows.shape[1]
    tm = _pick(slots, (1024, 512, 256, 128, 64, 32, 16, 8))
    tpe = slots // tm
    rows_f = rows.reshape(-1)
    hm = ffn_up(h, rows_f, w1, w3, tm, tpe)
    gates_b = jnp.broadcast_to(gates.reshape(-1, 1), (n_exp * slots, LANE))
    y = ffn_down(hm, w2, gates_b, tm, tpe)
    tok_sorted, src_sorted = lax.sort((rows_f, jnp.arange(n_exp * slots, dtype=I32)), num_keys=1)
    npairs = tok_sorted.shape[0]
    pad = (-npairs) % LANE
    tok_sorted = jnp.pad(tok_sorted, (0, pad), constant_values=-1)
    src_sorted = jnp.pad(src_sorted, (0, pad))
    starts = jnp.arange(t // TOK_TILE + 1, dtype=I32) * TOK_TILE
    bounds = jnp.searchsorted(tok_sorted[:npairs], starts, side="left").astype(I32)
    p0, p1 = bounds[:-1], bounds[1:]
    c0 = p0 // LANE
    nch = jnp.where(p1 > p0, (p1 + LANE - 1) // LANE - c0, 0).astype(I32)
    tok_chunks = tok_sorted.reshape(-1, 1, LANE)
    return moe_combine(x, y, src_sorted, tok_chunks, c0, nch, mods, cidx, 5)


def kernel(x_prompt, x_sample, cache_attn_k, cache_attn_v, state_ret_fwd, state_ret_bwd, c, c_ctx, ada_w, ada_b, norm1_g, norm2_g, ev_w_in, ev_w_out, hy_sconv_w, hy_sconv_b, hy_ffn_w1, hy_ffn_b1, hy_ffn_w2, hy_ffn_b2, hy_ffn_w3, hy_freq, hy_bias, at_q_norm, at_k_norm, at_sink, od_w_in, od_w_out, ret_decay, ret_gn, moe_router, moe_w1, moe_w3, moe_w2):
    bp, lp, d = x_prompt.shape
    bs, ls, _ = x_sample.shape
    depth = ada_w.shape[0]
    tp, ts = bp * lp, bs * ls
    t = tp + ts
    groups = ((0, bp, lp), (tp, bs, ls))
    a_width = hy_bias.shape[-1]
    hd = at_q_norm.shape[-1]
    n_heads = at_sink.shape[-1]
    n_kv = cache_attn_k.shape[3]
    c_heads, c_dv = ret_gn.shape[1], ret_gn.shape[2]
    c_dk = (od_w_in.shape[-1] - 2 * c_heads * c_dv) // (2 * c_heads)

    ntp, nts = tp // TOK_TILE, ts // TOK_TILE
    tps, tss = lp // TOK_TILE, ls // TOK_TILE
    tile = np.arange(ntp + nts)
    cidx = jnp.asarray(np.where(tile < ntp, 0, 1 + (tile - ntp) // tss), I32)
    in_seq = np.where(tile < ntp, tile % tps, (tile - ntp) % tss)
    seq_tiles = np.where(tile < ntp, tps, tss)
    first = jnp.asarray(in_seq == 0, I32)
    last = jnp.asarray(in_seq == seq_tiles - 1, I32)
    tm_mm = _pick(math.gcd(tp, ls), (1024, 512, 256))
    cidx_mm = cidx[::tm_mm // TOK_TILE]

    cond = jnp.concatenate([c_ctx[None, :], c], axis=0)
    cond = jnp.pad(cond, ((0, (-cond.shape[0]) % 8), (0, 0)))
    mods_all = adaln_all(cond, ada_w, ada_b)
    x = jnp.concatenate([x_prompt.reshape(tp, d), x_sample.reshape(ts, d)], axis=0)

    new_k, new_v, new_sf, new_sb = [], [], [], []
    for l in range(depth):
        mods = mods_all[l].reshape(-1, 6, 1, d)
        j = l // 2
        h = norm_mod(x, norm1_g[l], mods, cidx, 0, 1)
        if l % 2 == 0:
            u = matmul([h], ev_w_in[j], out_dtype=F32, tm=tm_mm)
            z, x0 = hy_pre(u, hy_sconv_w[j], hy_sconv_b[j], first, last, a_width)
            convs = []
            for row0, nb, seq in groups:
                hf, hb = hyena_filters(seq, hy_ffn_w1[j], hy_ffn_b1[j], hy_ffn_w2[j], hy_ffn_b2[j],
                                       hy_ffn_w3[j], hy_freq[j], a_width)
                conv_fn = long_conv_direct if seq <= 512 else long_conv_two_stage
                convs.append(conv_fn(z, row0, nb, seq, hf, hb))
            ya = hy_post(x0, z, hy_bias[j], convs[0], convs[1])
            cos_s, sin_s = _rope_tables(ls, hd)
            cos_t = jnp.concatenate([jnp.ones((tp, hd), F32), jnp.tile(cos_s, (bs, 1))], axis=0)
            sin_t = jnp.concatenate([jnp.zeros((tp, hd), F32), jnp.tile(sin_s, (bs, 1))], axis=0)
            q, k, v, k_normed = qkv_prep(u, 3 * a_width, n_heads, n_kv, hd, at_q_norm[j], at_k_norm[j], cos_t, sin_t)
            yb_p = attn_context(q, k, v, at_sink[j], 0, bp, lp, n_kv, hd)
            past = cache_attn_k.shape[2]
            ck = cache_attn_k[:, j].reshape(bs, past, n_kv * hd)
            cv = cache_attn_v[:, j].reshape(bs, past, n_kv * hd)
            yb_s = attn_latent(q, k, v, ck, cv, at_sink[j], tp, bs, ls, n_kv, hd)
            yb = jnp.concatenate([yb_p, yb_s], axis=0)
            x = matmul([ya, yb], ev_w_out[j], out_dtype=F32, tm=tm_mm, res=x, mods=mods, cidx=cidx_mm, k_gate=2)
            kv0 = 3 * a_width + n_heads * hd
            new_k.append(k_normed[:tp].reshape(bp, 1, lp, n_kv, hd))
            new_v.append(u[:tp, kv0 + n_kv * hd:kv0 + 2 * n_kv * hd].reshape(bp, 1, lp, n_kv, hd))
        else:
            qkvg = matmul([h], od_w_in[j], out_dtype=BF16, tm=tm_mm)
            log_gamma = -jnp.exp(ret_decay[j].astype(F32))
            y_p, sf, sb = retention(qkvg, 0, bp, lp, c_heads, c_dk, c_dv, log_gamma, ret_gn[j], emit_state=True)
            (y_s,) = retention(qkvg, tp, bs, ls, c_heads, c_dk, c_dv, log_gamma, ret_gn[j],
                               s_f0=state_ret_fwd[:, j:j + 1], s_b0=state_ret_bwd[:, j:j + 1])
            y = jnp.concatenate([y_p, y_s], axis=0)
            x = matmul([y], od_w_out[j], out_dtype=F32, tm=tm_mm, tn=256, res=x, mods=mods, cidx=cidx_mm, k_gate=2)
            new_sf.append(sf)
            new_sb.append(sb)
        x = ec_moe(x, norm2_g[l], mods, cidx, moe_router[l], moe_w1[l], moe_w3[l], moe_w2[l], groups)

    cat = lambda parts: jnp.concatenate(parts, axis=1)
    return (x[:tp].reshape(bp, lp, d), x[tp:].reshape(bs, ls, d), cat(new_k), cat(new_v), cat(new_sf), cat(new_sb))
```

```python
import functools
import math

import ml_dtypes
import numpy as np
import jax
import jax.numpy as jnp
from jax import lax
from jax.experimental import pallas as pl
from jax.experimental.pallas import tpu as pltpu

F32 = jnp.float32
BF16 = jnp.bfloat16
I32 = jnp.int32
HIGHEST = lax.Precision.HIGHEST

EPS = 1e-6
NEG_INF = -1e30
GRID_W = 64
WINDOW = 128
ATT_BLOCK = 128
RET_CHUNK = 128
ROPE_BASE = 10000.0
A_BANDS = 16
A_SHORT_DECAY_PCT = 0.3
A_LONG_DECAY_PCT = 1.5
A_DECAY_TARGET = 1e-2
CAPACITY_FACTOR = 2

TOK_TILE = 256
LANE = 128
FFT_N2 = 128
VMEM_LIMIT = 56 * 1024 * 1024


def _pick(n, cands):
    for c in cands:
        if n % c == 0:
            return c
    raise ValueError(f"no tile for {n} in {cands}")


def _call(kernel, *, grid, in_specs, out_specs, out_shape, scratch=(), nsp=0, name=None):
    gs = pltpu.PrefetchScalarGridSpec(num_scalar_prefetch=nsp, grid=grid, in_specs=in_specs,
                                      out_specs=out_specs, scratch_shapes=list(scratch))
    cp = pltpu.CompilerParams(dimension_semantics=("arbitrary",) * len(grid), vmem_limit_bytes=VMEM_LIMIT)
    return pl.pallas_call(kernel, grid_spec=gs, out_shape=out_shape, compiler_params=cp, name=name)


def _silu(x):
    return x / (1.0 + jnp.exp(-x))


def _bdot(a, b):
    return jnp.dot(a, b, preferred_element_type=F32)


def _split(x):
    hi = x.astype(BF16)
    lo = (x - hi.astype(F32)).astype(BF16)
    return hi, lo


def _dot3(mh, ml, x):
    xh, xl = _split(x)
    return _bdot(mh, xh) + _bdot(mh, xl) + _bdot(ml, xh)


def _adaln_kernel(c_ref, w_ref, b_ref, o_ref):
    s = _silu(c_ref[...])
    o_ref[0] = _bdot(s.astype(BF16), w_ref[0].astype(BF16)) + b_ref[0]


def adaln_all(cond, ada_w, ada_b):
    depth, d, n6 = ada_w.shape
    rows = cond.shape[0]
    tn = _pick(n6, (1024, 512, 256, 128))
    return _call(
        _adaln_kernel, grid=(depth, n6 // tn),
        in_specs=[pl.BlockSpec((rows, d), lambda l, j: (0, 0)),
                  pl.BlockSpec((1, d, tn), lambda l, j: (l, 0, j)),
                  pl.BlockSpec((1, 1, tn), lambda l, j: (l, 0, j))],
        out_specs=pl.BlockSpec((1, rows, tn), lambda l, j: (l, 0, j)),
        out_shape=jax.ShapeDtypeStruct((depth, rows, n6), F32), name="adaln",
    )(cond, ada_w, ada_b.reshape(depth, 1, n6))


def _row_sources(a, tm):
    if not isinstance(a, tuple):
        return [a], [lambda i: i], None
    a0, a1 = a
    n0, n1 = a0.shape[0] // tm, a1.shape[0] // tm
    return [a0, a1], [lambda i: jnp.minimum(i, n0 - 1), lambda i: jnp.clip(i - n0, 0, n1 - 1)], n0


def _load_rows(refs, i, split):
    if split is None:
        return refs[0][...]
    return jnp.where(i < split, refs[0][...], refs[1][...])


def _norm_mod(x, g_ref, sc_ref, sh_ref):
    ms = jnp.mean(x * x, axis=-1, keepdims=True)
    y = x * lax.rsqrt(ms + EPS) * g_ref[...]
    return y * (1.0 + sc_ref[0, 0]) + sh_ref[0, 0]


def _norm_mod_kernel(cidx_ref, *refs, nsrc, split):
    g_ref, sc_ref, sh_ref, o_ref = refs[nsrc:]
    x = _load_rows(refs[:nsrc], pl.program_id(0), split)
    o_ref[...] = _norm_mod(x, g_ref, sc_ref, sh_ref).astype(o_ref.dtype)


def _norm_router_kernel(cidx_ref, *refs, nsrc, split):
    g_ref, sc_ref, sh_ref, rt_ref, h_ref, aff_ref = refs[nsrc:]
    x = _load_rows(refs[:nsrc], pl.program_id(0), split)
    h = _norm_mod(x, g_ref, sc_ref, sh_ref)
    h_ref[...] = h
    logits = lax.dot_general(rt_ref[...], h, (((1,), (1,)), ((), ())), precision=HIGHEST,
                             preferred_element_type=F32)
    m = jnp.max(logits, axis=0, keepdims=True)
    e = jnp.exp(logits - m)
    aff_ref[...] = e / jnp.sum(e, axis=0, keepdims=True)


def norm_mod(x, gain, mods, cidx, k_shift, k_scale, *, router_t=None):
    tm = TOK_TILE
    srcs, fns, split = _row_sources(x, tm)
    t = sum(a.shape[0] for a in srcs)
    d = srcs[0].shape[1]
    specs = [pl.BlockSpec((tm, d), lambda i, c, f=f: (f(i), 0)) for f in fns]
    specs += [pl.BlockSpec((1, d), lambda i, c: (0, 0)),
              pl.BlockSpec((1, 1, 1, d), lambda i, c: (c[i], k_scale, 0, 0)),
              pl.BlockSpec((1, 1, 1, d), lambda i, c: (c[i], k_shift, 0, 0))]
    kw = dict(nsrc=len(srcs), split=split)
    if router_t is None:
        return _call(functools.partial(_norm_mod_kernel, **kw), grid=(t // tm,), nsp=1, in_specs=specs,
                     out_specs=pl.BlockSpec((tm, d), lambda i, c: (i, 0)),
                     out_shape=jax.ShapeDtypeStruct((t, d), BF16), name="norm_mod",
                     )(cidx, *srcs, gain.reshape(1, d), mods, mods)
    e = router_t.shape[0]
    specs.append(pl.BlockSpec((e, d), lambda i, c: (0, 0)))
    return _call(functools.partial(_norm_router_kernel, **kw), grid=(t // tm,), nsp=1, in_specs=specs,
                 out_specs=[pl.BlockSpec((tm, d), lambda i, c: (i, 0)),
                            pl.BlockSpec((e, tm), lambda i, c: (0, i))],
                 out_shape=[jax.ShapeDtypeStruct((t, d), F32), jax.ShapeDtypeStruct((e, t), F32)],
                 name="norm_router",
                 )(cidx, *srcs, gain.reshape(1, d), mods, mods, router_t)


def _mm_kernel(*refs, xcounts, xsplits, gated, rcount, rsplit):
    pos = 1 if gated else 0
    i = pl.program_id(1)
    x_refs = []
    for cnt in xcounts:
        x_refs.append(refs[pos:pos + cnt])
        pos += cnt
    nx = len(xcounts)
    w_refs = refs[pos:pos + nx]
    pos += nx
    if gated:
        res_refs, gate_ref = refs[pos:pos + rcount], refs[pos + rcount]
        pos += rcount + 1
    o_ref = refs[pos]
    wb_refs = refs[pos + 1:]

    @pl.when(i == 0)
    def _():
        for w_ref, wb in zip(w_refs, wb_refs):
            wb[...] = w_ref[...].astype(BF16)

    acc = None
    for xr, split, wb in zip(x_refs, xsplits, wb_refs):
        part = _bdot(_load_rows(xr, i, split), wb[...])
        acc = part if acc is None else acc + part
    if gated:
        acc = _load_rows(res_refs, i, rsplit) + gate_ref[0, 0] * acc
    o_ref[...] = acc.astype(o_ref.dtype)


def matmul(xs, w, *, out_dtype, tm, tn=512, col0=0, ncols=None, res=None, mods=None, cidx=None, k_gate=None):
    srcs = [_row_sources(x, tm) for x in xs]
    m = sum(a.shape[0] for a in srcs[0][0])
    kk = srcs[0][0][0].shape[1]
    nx = len(xs)
    assert w.shape[0] == nx * kk
    n = w.shape[1] - col0 if ncols is None else ncols
    tn = _pick(math.gcd(n, col0) if col0 else n, (tn, 512, 256, 128))
    cb0 = col0 // tn
    gated = res is not None
    if gated:
        imap = lambda f: (lambda j, i, c: f(j, i, c))
    else:
        imap = lambda f: (lambda j, i: f(j, i, None))
    specs, args = [], []
    for arrs, fns, _ in srcs:
        specs += [pl.BlockSpec((tm, kk), imap(lambda j, i, c, f=f: (f(i), 0))) for f in fns]
        args += arrs
    specs += [pl.BlockSpec((kk, tn), imap(lambda j, i, c, q=q: (q, cb0 + j))) for q in range(nx)]
    args += [w] * nx
    rcount, rsplit = 0, None
    if gated:
        assert mods.shape[-1] == n and col0 == 0
        rarrs, rfns, rsplit = _row_sources(res, tm)
        rcount = len(rarrs)
        specs += [pl.BlockSpec((tm, tn), imap(lambda j, i, c, f=f: (f(i), j))) for f in rfns]
        specs.append(pl.BlockSpec((1, 1, 1, tn), imap(lambda j, i, c: (c[i], k_gate, 0, j))))
        args = [cidx] + args + rarrs + [mods]
    kern = functools.partial(_mm_kernel, xcounts=tuple(len(s[0]) for s in srcs),
                             xsplits=tuple(s[2] for s in srcs), gated=gated, rcount=rcount, rsplit=rsplit)
    return _call(kern, grid=(n // tn, m // tm), nsp=1 if gated else 0, in_specs=specs,
                 out_specs=pl.BlockSpec((tm, tn), imap(lambda j, i, c: (i, j))),
                 out_shape=jax.ShapeDtypeStruct((m, n), out_dtype),
                 scratch=[pltpu.VMEM((kk, tn), BF16) for _ in xs], name="matmul",
                 )(*args)


def _mm_t_kernel(x_ref, w_ref, o_ref, wb):
    @pl.when(pl.program_id(1) == 0)
    def _():
        wb[...] = w_ref[...].astype(BF16)

    o_ref[...] = lax.dot_general(wb[...], x_ref[...], (((1,), (1,)), ((), ())),
                                 preferred_element_type=F32).astype(o_ref.dtype)


def matmul_t(x, wt, *, out_dtype, tm, tn=512):
    m, kk = x.shape
    n = wt.shape[0]
    tn = _pick(n, (tn, 512, 256, 128))
    return _call(_mm_t_kernel, grid=(n // tn, m // tm),
                 in_specs=[pl.BlockSpec((tm, kk), lambda j, i: (i, 0)), pl.BlockSpec((tn, kk), lambda j, i: (j, 0))],
                 out_specs=pl.BlockSpec((tn, tm), lambda j, i: (j, i)),
                 out_shape=jax.ShapeDtypeStruct((n, m), out_dtype),
                 scratch=[pltpu.VMEM((tn, kk), BF16)], name="matmul_t")(x, wt)


def _hy_pre_kernel(first_ref, last_ref, *refs):
    ins = refs[:15]
    z_ref, x0_ref = refs[15], refs[16]
    i = pl.program_id(0)
    is_first = first_ref[i] == 1
    is_last = last_ref[i] == 1

    def sconv(m_ref, p_ref, n_ref, w_ref, b_ref):
        x = m_ref[...]
        tl = x.shape[0]
        rows = lax.broadcasted_iota(I32, x.shape, 0)
        prev_row = jnp.where(is_first, 0.0, p_ref[7:8, :])
        next_row = jnp.where(is_last, 0.0, n_ref[0:1, :])
        xm1 = jnp.where(rows == 0, prev_row, pltpu.roll(x, 1, 0))
        xp1 = jnp.where(rows == tl - 1, next_row, pltpu.roll(x, tl - 1, 0))
        w = w_ref[...]
        return xm1 * w[0:1] + x * w[1:2] + xp1 * w[2:3] + b_ref[...]

    hv = sconv(ins[0], ins[1], ins[2], ins[9], ins[12])
    x0 = sconv(ins[3], ins[4], ins[5], ins[10], ins[13])
    x1 = sconv(ins[6], ins[7], ins[8], ins[11], ins[14])
    z_ref[...] = x1 * hv
    x0_ref[...] = x0


def hy_pre(u, sc_w, sc_b, first, last, a_width):
    t = u.shape[0]
    tl = TOK_TILE
    cb = _pick(a_width, (512, 256, 128))
    ncb = a_width // cb
    nt8 = t // 8
    specs = []
    for part in range(3):
        off = part * ncb
        specs += [pl.BlockSpec((tl, cb), lambda i, j, f, l, off=off: (i, off + j)),
                  pl.BlockSpec((8, cb), lambda i, j, f, l, off=off: (jnp.maximum(i * (tl // 8) - 1, 0), off + j)),
                  pl.BlockSpec((8, cb), lambda i, j, f, l, off=off: (jnp.minimum((i + 1) * (tl // 8), nt8 - 1), off + j))]
    for part in range(3):
        off = part * ncb
        specs.append(pl.BlockSpec((3, cb), lambda i, j, f, l, off=off: (0, off + j)))
    for part in range(3):
        off = part * ncb
        specs.append(pl.BlockSpec((1, cb), lambda i, j, f, l, off=off: (0, off + j)))
    out_spec = pl.BlockSpec((tl, cb), lambda i, j, f, l: (i, j))
    scb = sc_b.reshape(1, -1)
    return _call(_hy_pre_kernel, grid=(t // tl, ncb), nsp=2, in_specs=specs,
                 out_specs=[out_spec, out_spec],
                 out_shape=[jax.ShapeDtypeStruct((t, a_width), F32)] * 2, name="hy_pre",
                 )(first, last, *([u] * 9), sc_w, sc_w, sc_w, scb, scb, scb)


def _filter_kernel(fa_ref, fb_ref, w1_ref, b1_ref, w2_ref, b2_ref, fr_ref, w3f_ref, w3b_ref, dl_ref, o_ref):
    hdot = functools.partial(jnp.dot, precision=HIGHEST, preferred_element_type=F32)

    def taps(feats, w3_ref):
        h = jnp.sin(fr_ref[0:1, :] * (hdot(feats, w1_ref[...]) + b1_ref[...]))
        h = jnp.sin(fr_ref[1:2, :] * (hdot(h, w2_ref[...]) + b2_ref[...]))
        return hdot(h, w3_ref[...]) * jnp.exp(-feats[:, 0:1] * dl_ref[...])

    hf = taps(fa_ref[...], w3f_ref)
    hb = taps(fb_ref[...], w3b_ref)
    s = (jnp.sum(jnp.abs(hf), axis=0, keepdims=True) + jnp.sum(jnp.abs(hb), axis=0, keepdims=True)) + EPS
    o_ref[0] = hf / s
    o_ref[1] = jnp.where(lax.broadcasted_iota(I32, hb.shape, 0) == 0, 0.0, hb / s)


def hyena_taps(seq, w1, b1, w2, b2, w3, freq, a_width):
    t = jnp.arange(seq, dtype=F32) / seq
    bands = jnp.linspace(1e-4, A_BANDS - 1, A_BANDS, dtype=F32)
    ang = 2.0 * math.pi * t[:, None] * bands[None, :]
    feats = jnp.concatenate([t[:, None], jnp.cos(ang), -jnp.sin(ang)], axis=-1)
    emb = feats.shape[1]
    feats = jnp.pad(feats, ((0, 0), (0, LANE - emb)))
    feats_b = jnp.roll(feats[::-1], 1, axis=0)
    w1p = jnp.pad(w1, ((0, LANE - emb), (0, 0)))
    max_decay = math.log(A_DECAY_TARGET) / A_SHORT_DECAY_PCT
    min_decay = math.log(A_DECAY_TARGET) / A_LONG_DECAY_PCT
    deltas = jnp.abs(jnp.linspace(min_decay, max_decay, a_width, dtype=F32)).reshape(1, a_width)
    ffn = w2.shape[0]
    cb = _pick(a_width, (256, 128))
    ncb = a_width // cb
    full = lambda shape: pl.BlockSpec(shape, lambda j: (0,) * len(shape))
    taps = _call(_filter_kernel, grid=(ncb,),
                 in_specs=[full((seq, LANE)), full((seq, LANE)), full((LANE, ffn)), full((1, ffn)), full((ffn, ffn)),
                           full((1, ffn)), full((2, ffn)),
                           pl.BlockSpec((ffn, cb), lambda j: (0, j)),
                           pl.BlockSpec((ffn, cb), lambda j: (0, ncb + j)),
                           pl.BlockSpec((1, cb), lambda j: (0, j))],
                 out_specs=pl.BlockSpec((2, seq, cb), lambda j: (0, 0, j)),
                 out_shape=jax.ShapeDtypeStruct((2, seq, a_width), F32), name="hy_filter",
                 )(feats, feats_b, w1p, b1.reshape(1, ffn), w2, b2.reshape(1, ffn), freq, w3, w3, deltas)
    return taps.reshape(2 * seq, a_width)


def _hilo(x):
    x32 = np.asarray(x, np.float32)
    hi = x32.astype(ml_dtypes.bfloat16)
    lo = (x32 - hi.astype(np.float32)).astype(ml_dtypes.bfloat16)
    return hi, lo


@functools.lru_cache(maxsize=None)
def _dft_direct_consts(seq):
    n = 2 * seq
    k = np.arange(n)
    ang = 2.0 * np.pi * ((k[:, None] * k[None, :]) % n) / n
    c, s = np.cos(ang), np.sin(ang)
    fwd = np.block([[c[:, :seq], s[:, :seq]], [-s[:, :seq], c[:, :seq]]])
    taps = np.concatenate([c, -s], axis=0)
    inv = np.block([[c[:seq, :], -s[:seq, :]], [s[:seq, :], c[:seq, :]]]) / n
    return _hilo(fwd), _hilo(taps), _hilo(inv)


@functools.lru_cache(maxsize=None)
def _dft_two_stage_consts(seq):
    n = 2 * seq
    n2c = FFT_N2
    n1c = n // n2c
    h = n1c // 2
    n2 = np.arange(n2c)[:, None, None]
    k1 = np.arange(n1c)[None, :, None]
    n1 = np.arange(n1c)[None, None, :]
    psi = 2.0 * np.pi * ((n1 * k1 * n2c + n2 * k1) % n) / n
    c, s = np.cos(psi), np.sin(psi)
    m1d = np.concatenate([np.concatenate([c[:, :, :h], s[:, :, :h]], axis=2),
                          np.concatenate([-s[:, :, :h], c[:, :, :h]], axis=2)], axis=1)
    m1t = np.concatenate([c, -s], axis=1)
    ct = np.swapaxes(c, 1, 2)[:, :h, :]
    st = np.swapaxes(s, 1, 2)[:, :h, :]
    m1i = np.concatenate([np.concatenate([ct, -st], axis=2),
                          np.concatenate([st, ct], axis=2)], axis=1) / n
    kk = np.arange(n2c)
    th = 2.0 * np.pi * ((kk[:, None] * kk[None, :]) % n2c) / n2c
    c2, s2 = np.cos(th), np.sin(th)
    m2 = np.block([[c2, s2], [-s2, c2]])
    m2i = np.block([[c2, -s2], [s2, c2]])
    return _hilo(m1d), _hilo(m1t), _hilo(m1i), _hilo(m2), _hilo(m2i)


def _cmul_stacked(x, hs, half):
    xr, xi = x[:half], x[half:]
    hr, hi = hs[:half], hs[half:]
    return jnp.concatenate([xr * hr - xi * hi, xr * hi + xi * hr], axis=0)


def _const_mm3_kernel(mh_ref, ml_ref, x_ref, o_ref):
    o_ref[...] = _dot3(mh_ref[...], ml_ref[...], x_ref[...])


def const_mm3(mh, ml, x):
    m, k = mh.shape
    c = x.shape[1]
    cb = _pick(c, (256, 128))
    return _call(_const_mm3_kernel, grid=(c // cb,),
                 in_specs=[pl.BlockSpec((m, k), lambda j: (0, 0)), pl.BlockSpec((m, k), lambda j: (0, 0)),
                           pl.BlockSpec((k, cb), lambda j: (0, j))],
                 out_specs=pl.BlockSpec((m, cb), lambda j: (0, j)),
                 out_shape=jax.ShapeDtypeStruct((m, c), F32), name="dft_taps")(mh, ml, x)


def _conv_direct_kernel(fh_ref, fl_ref, gh_ref, gl_ref, zr_ref, zi_ref, h_ref, yr_ref, yi_ref, *, seq):
    x = jnp.concatenate([zr_ref[...], zi_ref[...]], axis=0)
    spec = _dot3(fh_ref[...], fl_ref[...], x)
    y = _dot3(gh_ref[...], gl_ref[...], _cmul_stacked(spec, h_ref[...], 2 * seq))
    yr_ref[...] = y[:seq]
    yi_ref[...] = y[seq:]


def long_conv_direct(z, row0, nb, seq, taps):
    a_width = z.shape[1]
    (fh, fl), (th, tl_), (gh, gl) = _dft_direct_consts(seq)
    spec_h = const_mm3(jnp.asarray(th), jnp.asarray(tl_), taps)
    cb = _pick(a_width, (256, 128))
    half = nb // 2
    b0 = row0 // seq
    full = lambda a: pl.BlockSpec(a.shape, lambda j, p: (0, 0))
    yr, yi = _call(
        functools.partial(_conv_direct_kernel, seq=seq), grid=(a_width // cb, half),
        in_specs=[full(fh), full(fl), full(gh), full(gl),
                  pl.BlockSpec((seq, cb), lambda j, p: (b0 + p, j)),
                  pl.BlockSpec((seq, cb), lambda j, p: (b0 + half + p, j)),
                  pl.BlockSpec((4 * seq, cb), lambda j, p: (0, j))],
        out_specs=[pl.BlockSpec((seq, cb), lambda j, p: (p, j))] * 2,
        out_shape=[jax.ShapeDtypeStruct((half * seq, a_width), F32)] * 2, name="conv_direct",
    )(jnp.asarray(fh), jnp.asarray(fl), jnp.asarray(gh), jnp.asarray(gl), z, z, spec_h)
    return jnp.concatenate([yr, yi], axis=0)


def _fft_s1_kernel(mh_ref, ml_ref, *refs, g, packed):
    o_ref = refs[-1]
    for j in range(g):
        if packed:
            x = jnp.concatenate([refs[0][0, j], refs[1][0, j]], axis=0)
        else:
            x = refs[0][0, j]
        o_ref[0, j] = _dot3(mh_ref[j], ml_ref[j], x)


def _fft_s1(mh, ml, zp, *, packed):
    n2c, m_rows, n1c = mh.shape
    bsz, _, rows, c = zp.shape
    p = bsz // 2 if packed else bsz
    cb = _pick(c, (256, 128))
    g = 16
    cspec = pl.BlockSpec((g, m_rows, n1c), lambda q, j, t: (t, 0, 0))
    zspecs = [pl.BlockSpec((1, g, rows, cb), lambda q, j, t: (q, t, 0, j))]
    args = [zp]
    if packed:
        zspecs.append(pl.BlockSpec((1, g, rows, cb), lambda q, j, t: (q + p, t, 0, j)))
        args.append(zp)
    return _call(functools.partial(_fft_s1_kernel, g=g, packed=packed), grid=(p, c // cb, n2c // g),
                 in_specs=[cspec, cspec] + zspecs,
                 out_specs=pl.BlockSpec((1, g, m_rows, cb), lambda q, j, t: (q, t, 0, j)),
                 out_shape=jax.ShapeDtypeStruct((p, n2c, m_rows, c), F32), name="fft_s1",
                 )(jnp.asarray(mh), jnp.asarray(ml), *args)


def _fft_s2_kernel(mh_ref, ml_ref, ih_ref, il_ref, *refs, g, conv):
    a_ref, o_ref = refs[0], refs[-1]
    half = a_ref.shape[2] // 2
    for j in range(g):
        spec = _dot3(mh_ref[...], ml_ref[...], a_ref[0, j])
        if conv:
            spec = _dot3(ih_ref[...], il_ref[...], _cmul_stacked(spec, refs[1][0, j], half))
        o_ref[0, j] = spec


def _fft_s2(m2, m2i, at, spec_h=None):
    p, n1c, rows, c = at.shape
    cb = _pick(c, (256, 128))
    g = _pick(n1c, (8, 4, 2, 1))
    conv = spec_h is not None
    mspec = pl.BlockSpec((rows, rows), lambda q, j, t: (0, 0))
    blk = pl.BlockSpec((1, g, rows, cb), lambda q, j, t: (q, t, 0, j))
    specs = [mspec] * 4 + [blk]
    args = [jnp.asarray(m2[0]), jnp.asarray(m2[1]), jnp.asarray(m2i[0]), jnp.asarray(m2i[1]), at]
    if conv:
        specs.append(pl.BlockSpec((1, g, rows, cb), lambda q, j, t: (0, t, 0, j)))
        args.append(spec_h)
    return _call(functools.partial(_fft_s2_kernel, g=g, conv=conv), grid=(p, c // cb, n1c // g),
                 in_specs=specs, out_specs=blk, out_shape=jax.ShapeDtypeStruct(at.shape, F32), name="fft_s2",
                 )(*args)


def _fft_s3_kernel(mh_ref, ml_ref, b_ref, yr_ref, yi_ref, *, g):
    half = yr_ref.shape[2]
    for j in range(g):
        y = _dot3(mh_ref[j], ml_ref[j], b_ref[0, j])
        yr_ref[0, j] = y[:half]
        yi_ref[0, j] = y[half:]


def _fft_s3(mh, ml, bt):
    n2c, n1c, rows = mh.shape
    p, _, _, c = bt.shape
    cb = _pick(c, (256, 128))
    g = 16
    h = n1c // 2
    oblk = pl.BlockSpec((1, g, h, cb), lambda q, j, t: (q, t, 0, j))
    cspec = pl.BlockSpec((g, n1c, rows), lambda q, j, t: (t, 0, 0))
    return _call(functools.partial(_fft_s3_kernel, g=g), grid=(p, c // cb, n2c // g),
                 in_specs=[cspec, cspec, pl.BlockSpec((1, g, rows, cb), lambda q, j, t: (q, t, 0, j))],
                 out_specs=[oblk, oblk],
                 out_shape=[jax.ShapeDtypeStruct((p, n2c, h, c), F32)] * 2, name="fft_s3",
                 )(jnp.asarray(mh), jnp.asarray(ml), bt)


def _swap_digits(a):
    p, x, y2, c = a.shape
    y = y2 // 2
    return a.reshape(p, x, 2, y, c).transpose(0, 3, 2, 1, 4).reshape(p, y, 2 * x, c)


def long_conv_two_stage(z, row0, nb, seq, taps):
    a_width = z.shape[1]
    m1d, m1t, m1i, m2, m2i = _dft_two_stage_consts(seq)
    n2c = FFT_N2
    n1c = 2 * seq // n2c
    taps_p = taps.reshape(1, n1c, n2c, a_width).transpose(0, 2, 1, 3)
    spec_h = _fft_s2(m2, m2i, _swap_digits(_fft_s1(m1t[0], m1t[1], taps_p, packed=False)))
    zs = lax.slice_in_dim(z, row0, row0 + nb * seq, axis=0)
    zp = zs.reshape(nb, n1c // 2, n2c, a_width).transpose(0, 2, 1, 3)
    a = _swap_digits(_fft_s1(m1d[0], m1d[1], zp, packed=True))
    b = _swap_digits(_fft_s2(m2, m2i, a, spec_h))
    yr, yi = _fft_s3(m1i[0], m1i[1], b)
    y = jnp.concatenate([yr, yi], axis=0)
    return y.transpose(0, 2, 1, 3).reshape(nb * seq, a_width)


def _hy_post_kernel(x0_ref, z_ref, bias_ref, cp_ref, cs_ref, o_ref, *, ntp):
    conv = jnp.where(pl.program_id(0) < ntp, cp_ref[...], cs_ref[...])
    o_ref[...] = (x0_ref[...] * (conv + z_ref[...] * bias_ref[...])).astype(o_ref.dtype)


def hy_post(x0, z, bias, conv_p, conv_s):
    t, a_width = x0.shape
    tl = TOK_TILE
    ntp = conv_p.shape[0] // tl
    nts = conv_s.shape[0] // tl
    cb = _pick(a_width, (512, 256, 128))
    blk = pl.BlockSpec((tl, cb), lambda i, j: (i, j))
    return _call(functools.partial(_hy_post_kernel, ntp=ntp), grid=(t // tl, a_width // cb),
                 in_specs=[blk, blk, pl.BlockSpec((1, cb), lambda i, j: (0, j)),
                           pl.BlockSpec((tl, cb), lambda i, j: (jnp.minimum(i, ntp - 1), j)),
                           pl.BlockSpec((tl, cb), lambda i, j: (jnp.clip(i - ntp, 0, nts - 1), j))],
                 out_specs=blk, out_shape=jax.ShapeDtypeStruct((t, a_width), BF16), name="hy_post",
                 )(x0, z, bias.reshape(1, a_width), conv_p, conv_s)


def _qkv_prep_kernel(q_ref, k_ref, v_ref, qn_ref, kn_ref, cos_ref, sin_ref, qo_ref, ko_ref, vo_ref, kc_ref, *, hd, scale):
    cos = cos_ref[...]
    sin = sin_ref[...]
    lane = lax.broadcasted_iota(I32, cos.shape, 1)
    low = (lane % (hd // 2)) < (hd // 4)

    def norm(x, g):
        return x * lax.rsqrt(jnp.mean(x * x, axis=-1, keepdims=True) + EPS) * g

    def rope(x):
        partner = jnp.where(low, pltpu.roll(x, hd - hd // 4, 1), pltpu.roll(x, hd // 4, 1))
        return x * cos + partner * sin

    for h in range(q_ref.shape[1] // hd):
        sl = slice(h * hd, (h + 1) * hd)
        qo_ref[:, sl] = (rope(norm(q_ref[:, sl], qn_ref[...])) * scale).astype(qo_ref.dtype)
    for h in range(k_ref.shape[1] // hd):
        sl = slice(h * hd, (h + 1) * hd)
        kn = norm(k_ref[:, sl], kn_ref[...])
        kc_ref[:, sl] = kn
        ko_ref[:, sl] = rope(kn).astype(ko_ref.dtype)
    vo_ref[...] = v_ref[...].astype(vo_ref.dtype)


def qkv_prep(u, col0, n_heads, n_kv, hd, qn, kn, cos_t, sin_t):
    t = u.shape[0]
    tl = TOK_TILE
    qw, kw = n_heads * hd, n_kv * hd
    assert col0 % qw == 0 and (col0 + qw) % kw == 0
    qb, kb = col0 // qw, (col0 + qw) // kw
    row = lambda w: pl.BlockSpec((tl, w), lambda i: (i, 0))
    vec = pl.BlockSpec((1, hd), lambda i: (0, 0))
    return _call(functools.partial(_qkv_prep_kernel, hd=hd, scale=hd ** -0.5), grid=(t // tl,),
                 in_specs=[pl.BlockSpec((tl, qw), lambda i: (i, qb)),
                           pl.BlockSpec((tl, kw), lambda i: (i, kb)),
                           pl.BlockSpec((tl, kw), lambda i: (i, kb + 1)),
                           vec, vec, row(hd), row(hd)],
                 out_specs=[row(qw), row(kw), row(kw), row(kw)],
                 out_shape=[jax.ShapeDtypeStruct((t, qw), BF16), jax.ShapeDtypeStruct((t, kw), BF16),
                            jax.ShapeDtypeStruct((t, kw), BF16), jax.ShapeDtypeStruct((t, kw), F32)],
                 name="qkv_prep")(u, u, u, qn.reshape(1, hd), kn.reshape(1, hd), cos_t, sin_t)


def _rope_tables(seq, hd):
    rows = seq // GRID_W
    row = jnp.repeat(jnp.arange(rows), GRID_W).astype(F32)
    col = jnp.tile(jnp.arange(GRID_W), rows).astype(F32)
    nf = hd // 4
    inv = jnp.exp(-math.log(ROPE_BASE) * jnp.arange(nf, dtype=F32) / nf)
    ang_r = row[:, None] * inv[None, :]
    ang_c = col[:, None] * inv[None, :]
    cos = jnp.concatenate([jnp.cos(ang_r)] * 2 + [jnp.cos(ang_c)] * 2, axis=-1)
    sin = jnp.concatenate([-jnp.sin(ang_r), jnp.sin(ang_r), -jnp.sin(ang_c), jnp.sin(ang_c)], axis=-1)
    return cos, sin


def _attn_kernel(sink_ref, q_ref, *refs, kinds, group, hd, nblk):
    nk = len(kinds)
    k_refs, v_refs, o_ref = refs[:nk], refs[nk:2 * nk], refs[2 * nk]
    kvg = pl.program_id(1)
    i = pl.program_id(2)
    tq = q_ref.shape[0]
    q = jnp.concatenate([q_ref[:, g * hd:(g + 1) * hd] for g in range(group)], axis=0)
    rows = lax.broadcasted_iota(I32, (group * tq, 1), 0)
    sink = jnp.full((group * tq, 1), sink_ref[kvg * group], F32)
    for g in range(1, group):
        sink = jnp.where(rows >= g * tq, sink_ref[kvg * group + g], sink)

    def load(ref):
        x = ref[0] if len(ref.shape) == 3 else ref[...]
        return x.astype(BF16)

    scores = []
    m = sink
    for kind, k_ref in zip(kinds, k_refs):
        s = lax.dot_general(q, load(k_ref), (((1,), (1,)), ((), ())), preferred_element_type=F32)
        if kind in ("prev", "next"):
            r = lax.broadcasted_iota(I32, s.shape, 0) % tq
            c = lax.broadcasted_iota(I32, s.shape, 1)
            if kind == "prev":
                ok = c >= r + jnp.where(i > 0, 0, 2 * tq)
            else:
                ok = c <= r - jnp.where(i < nblk - 1, 0, 2 * tq)
            s = jnp.where(ok, s, NEG_INF)
        scores.append(s)
        m = jnp.maximum(m, jnp.max(s, axis=-1, keepdims=True))
    den = jnp.exp(sink - m)
    acc = None
    for s, v_ref in zip(scores, v_refs):
        p = jnp.exp(s - m)
        den = den + jnp.sum(p, axis=-1, keepdims=True)
        pv = _bdot(p.astype(BF16), load(v_ref))
        acc = pv if acc is None else acc + pv
    out = acc / den
    o_ref[...] = jnp.concatenate([out[g * tq:(g + 1) * tq] for g in range(group)], axis=1).astype(o_ref.dtype)


def attn_context(q, k, v, sink, row0, nb, seq, n_kv, hd):
    group = q.shape[1] // (n_kv * hd)
    b0 = row0 // seq
    kv = pl.BlockSpec((seq, hd), lambda b, g, i, s: (b0 + b, g))
    return _call(functools.partial(_attn_kernel, kinds=("all",), group=group, hd=hd, nblk=1),
                 grid=(nb, n_kv, 1), nsp=1,
                 in_specs=[pl.BlockSpec((seq, group * hd), lambda b, g, i, s: (b0 + b, g)), kv, kv],
                 out_specs=pl.BlockSpec((seq, group * hd), lambda b, g, i, s: (b, g)),
                 out_shape=jax.ShapeDtypeStruct((nb * seq, q.shape[1]), BF16), name="attn_context",
                 )(sink, q, k, v)


def attn_latent(q, k, v, ck, cv, sink, row0, nb, seq, n_kv, hd):
    group = q.shape[1] // (n_kv * hd)
    blk = ATT_BLOCK
    nblk = seq // blk
    r0 = row0 // blk
    past = ck.shape[1]
    cur = lambda b, g, i, s: (r0 + b * nblk + i, g)
    prev = lambda b, g, i, s: (r0 + b * nblk + jnp.maximum(i - 1, 0), g)
    nxt = lambda b, g, i, s: (r0 + b * nblk + jnp.minimum(i + 1, nblk - 1), g)
    ctx = pl.BlockSpec((1, past, hd), lambda b, g, i, s: (b, 0, g))
    kvs = [pl.BlockSpec((blk, hd), f) for f in (prev, cur, nxt)]
    return _call(functools.partial(_attn_kernel, kinds=("prev", "cur", "next", "ctx"), group=group, hd=hd, nblk=nblk),
                 grid=(nb, n_kv, nblk), nsp=1,
                 in_specs=[pl.BlockSpec((blk, group * hd), cur)] + kvs + [ctx] + kvs + [ctx],
                 out_specs=pl.BlockSpec((blk, group * hd), lambda b, g, i, s: (b * nblk + i, g)),
                 out_shape=jax.ShapeDtypeStruct((nb * seq, q.shape[1]), BF16), name="attn_latent",
                 )(sink, q, k, k, k, ck, v, v, v, cv)


def _ret_kernel(lg_ref, q_ref, kt_ref, v_ref, g_ref, gn_ref, *refs, nc, dk, has_init, emit_state):
    pos = 0
    if has_init:
        s0_refs = refs[0:2]
        pos = 2
    y_ref = refs[pos]
    pos += 1
    if emit_state:
        so_refs = refs[pos:pos + 2]
        pos += 2
    o_scr, s_scr = refs[pos], refs[pos + 1]
    h = pl.program_id(1)
    ch = RET_CHUNK
    kscale = dk ** -0.5
    ii = lax.broadcasted_iota(I32, (ch, ch), 0).astype(F32)
    jj = lax.broadcasted_iota(I32, (ch, ch), 1).astype(F32)
    col = lax.broadcasted_iota(I32, (ch, 1), 0).astype(F32)
    lane = lax.broadcasted_iota(I32, (1, ch), 1).astype(F32)
    gn = gn_ref[0]

    consts = []
    for d in range(2):
        lg = lg_ref[d, h]
        diff = ii - jj if d == 0 else jj - ii
        dmat = jnp.where(diff >= 0, jnp.exp(lg * jnp.maximum(diff, 0.0)), 0.0) * kscale
        xi = jnp.exp(lg * (col + 1.0)) if d == 0 else jnp.exp(lg * (ch - col))
        zeta = (jnp.exp(lg * (ch - 1.0 - lane)) if d == 0 else jnp.exp(lg * lane)) * kscale
        cdec = jnp.exp(lg * jnp.full((1, 1), float(ch), F32))
        consts.append((dmat, xi, zeta, cdec))
        if has_init:
            s_scr[d] = s0_refs[d][0, 0, 0]
        else:
            s_scr[d] = jnp.zeros(s_scr.shape[1:], F32)

    def chunk_out(d, c):
        dmat, xi, zeta, cdec = consts[d]
        r0 = pl.multiple_of(c * ch, ch)
        qc = q_ref[0, pl.ds(r0, ch), :]
        ktc = kt_ref[:, pl.ds(r0, ch)]
        vc = v_ref[0, pl.ds(r0, ch), :]
        sc = _bdot(qc, ktc) * dmat
        s = s_scr[d]
        o = _bdot(sc.astype(BF16), vc) + _bdot(qc, s.astype(BF16)) * xi
        kz = (ktc.astype(F32) * zeta).astype(BF16)
        s_scr[d] = s * cdec + _bdot(kz, vc)
        return r0, o

    def finalize(r0, o):
        mu = jnp.mean(o, axis=-1, keepdims=True)
        var = jnp.mean(jnp.square(o - mu), axis=-1, keepdims=True)
        on = (o - mu) * lax.rsqrt(var + EPS) * gn
        gate = _silu(g_ref[0, pl.ds(r0, ch), :].astype(F32))
        y_ref[0, pl.ds(r0, ch), :] = (gate * on).astype(y_ref.dtype)

    def first_half(n, carry):
        for d in range(2):
            r0, o = chunk_out(d, n if d == 0 else nc - 1 - n)
            o_scr[pl.ds(r0, ch), :] = o
        return carry

    def second_half(n, carry):
        for d in range(2):
            r0, o = chunk_out(d, n if d == 0 else nc - 1 - n)
            finalize(r0, o_scr[pl.ds(r0, ch), :] + o)
        return carry

    lax.fori_loop(0, nc // 2, first_half, 0)
    lax.fori_loop(nc // 2, nc, second_half, 0)
    if emit_state:
        for d in range(2):
            so_refs[d][0, 0, 0] = s_scr[d]


def retention(q, kt, vg, row0, nb, seq, n_heads, dk, dv, log_gamma, gn, *, s_f0=None, s_b0=None, emit_state=False):
    t = q.shape[0]
    q3 = q.reshape(t // seq, seq, n_heads * dk)
    vg3 = vg.reshape(t // seq, seq, 2 * n_heads * dv)
    b0 = row0 // seq
    has_init = s_f0 is not None
    specs = [pl.BlockSpec(memory_space=pltpu.SMEM),
             pl.BlockSpec((1, seq, dk), lambda b, h: (b0 + b, 0, h)),
             pl.BlockSpec((dk, seq), lambda b, h: (h, b0 + b)),
             pl.BlockSpec((1, seq, dv), lambda b, h: (b0 + b, 0, h)),
             pl.BlockSpec((1, seq, dv), lambda b, h: (b0 + b, 0, n_heads + h)),
             pl.BlockSpec((1, 1, dv), lambda b, h: (h, 0, 0))]
    args = [log_gamma, q3, kt, vg3, vg3, gn.reshape(n_heads, 1, dv)]
    st = pl.BlockSpec((1, 1, 1, dk, dv), lambda b, h: (b, 0, h, 0, 0))
    if has_init:
        specs += [st, st]
        args += [s_f0, s_b0]
    out_specs = [pl.BlockSpec((1, seq, dv), lambda b, h: (b, 0, h))]
    out_shape = [jax.ShapeDtypeStruct((nb, seq, n_heads * dv), BF16)]
    if emit_state:
        out_specs += [st, st]
        out_shape += [jax.ShapeDtypeStruct((nb, 1, n_heads, dk, dv), F32)] * 2
    gs = pl.GridSpec(grid=(nb, n_heads), in_specs=specs, out_specs=out_specs,
                     scratch_shapes=[pltpu.VMEM((seq, dv), F32), pltpu.VMEM((2, dk, dv), F32)])
    outs = pl.pallas_call(
        functools.partial(_ret_kernel, nc=seq // RET_CHUNK, dk=dk, has_init=has_init, emit_state=emit_state),
        grid_spec=gs, out_shape=out_shape, name="retention",
        compiler_params=pltpu.CompilerParams(dimension_semantics=("arbitrary", "arbitrary"),
                                             vmem_limit_bytes=VMEM_LIMIT))(*args)
    y = outs[0].reshape(nb * seq, n_heads * dv)
    return (y,) + tuple(outs[1:])


def _ffn_up_kernel(idx_ref, h_hbm, w1_ref, w3_ref, o_ref, xf_scr, xb_scr, sem, *, tm, nf):
    tile = pl.program_id(0)
    f = pl.program_id(1)
    slot = tile % 2
    per = tm // nf

    def row_copy(row, dst_slot, r):
        return pltpu.make_async_copy(h_hbm.at[pl.ds(row, 1)], xf_scr.at[dst_slot, pl.ds(r, 1)], sem.at[dst_slot])

    def wait_rows(src_slot):
        pltpu.make_async_copy(h_hbm.at[pl.ds(0, tm)], xf_scr.at[src_slot], sem.at[src_slot]).wait()

    @pl.when(jnp.logical_and(tile == 0, f == 0))
    def _():
        def issue(r, carry):
            row_copy(idx_ref[r], 0, r).start()
            return carry

        lax.fori_loop(0, tm, issue, 0)

    @pl.when(f == 0)
    def _():
        wait_rows(slot)
        xb_scr[...] = xf_scr[slot].astype(BF16)

    base = (tile + 1) * tm + f * per
    for r in range(per):
        row_copy(idx_ref[base + r], 1 - slot, f * per + r).start()

    x = xb_scr[...]
    a = _bdot(x, w1_ref[0, 0].astype(BF16))
    b = _bdot(x, w3_ref[0, 0].astype(BF16))
    o_ref[...] = (_silu(a) * b).astype(o_ref.dtype)

    @pl.when(jnp.logical_and(tile == pl.num_programs(0) - 1, f == nf - 1))
    def _():
        wait_rows(1 - slot)


def ffn_up(h, rows, w1, w3, layer, tm, tiles_per_expert):
    _, n_exp, d, ff = w1.shape
    nslots = rows.shape[0]
    tf = _pick(ff, (512, 256, 128))
    nf = ff // tf
    tpe = tiles_per_expert
    rows = jnp.pad(rows, (0, tm))
    wspec = pl.BlockSpec((1, 1, d, tf), lambda t, f, idx: (layer, t // tpe, 0, f))
    return _call(functools.partial(_ffn_up_kernel, tm=tm, nf=nf), grid=(nslots // tm, nf), nsp=1,
                 in_specs=[pl.BlockSpec(memory_space=pl.ANY), wspec, wspec],
                 out_specs=pl.BlockSpec((tm, tf), lambda t, f, idx: (t, f)),
                 out_shape=jax.ShapeDtypeStruct((nslots, ff), BF16),
                 scratch=[pltpu.VMEM((2, tm, d), F32), pltpu.VMEM((tm, d), BF16), pltpu.SemaphoreType.DMA((2,))],
                 name="ffn_up")(rows, h, w1, w3)


def _ffn_down_kernel(x_ref, w_ref, g_ref, t_ref, o_ref, *, nd):
    j = pl.program_id(1)
    nl = o_ref.shape[1] // LANE

    @pl.when(j < nd)
    def _():
        acc = _bdot(x_ref[...], w_ref[0, 0].astype(BF16))
        g = g_ref[...]
        for c in range(nl):
            o_ref[:, c * LANE:(c + 1) * LANE] = acc[:, c * LANE:(c + 1) * LANE] * g

    @pl.when(j == nd)
    def _():
        tok = t_ref[...]
        for c in range(nl):
            o_ref[:, c * LANE:(c + 1) * LANE] = tok


def ffn_down(hm, w2, layer, gates, toks, tm, tiles_per_expert):
    _, n_exp, ff, d = w2.shape
    nslots = hm.shape[0]
    tn = _pick(d, (512, 256, 128))
    nd = d // tn
    tpe = tiles_per_expert
    side = pl.BlockSpec((tm, LANE), lambda t, j: (t, 0))
    return _call(functools.partial(_ffn_down_kernel, nd=nd), grid=(nslots // tm, nd + 1),
                 in_specs=[pl.BlockSpec((tm, ff), lambda t, j: (t, 0)),
                           pl.BlockSpec((1, 1, ff, tn), lambda t, j: (layer, t // tpe, 0, jnp.minimum(j, nd - 1))),
                           side, side],
                 out_specs=pl.BlockSpec((tm, tn), lambda t, j: (t, j)),
                 out_shape=jax.ShapeDtypeStruct((nslots, d + tn), F32), name="ffn_down")(hm, w2, gates, toks)


SLOT_GROUP = 8
COMBINE_BATCH = 64


def _combine_kernel(flat_ref, pstart_ref, cidx_ref, y_hbm, x_ref, g_ref, *refs, d, nsplit):
    nout = 1 if nsplit is None else 2
    o_refs = refs[:nout]
    z_scr, acc_scr, sem = refs[nout:]
    i = pl.program_id(0)
    tl = x_ref.shape[0]
    gb = COMBINE_BATCH
    rows_b = gb * SLOT_GROUP
    p0 = pstart_ref[i]
    npos = pstart_ref[i + 1] - p0
    nbatch = (npos + gb - 1) // gb

    def gather(bi, slot):
        def issue(k, carry):
            g = flat_ref[p0 + bi * gb + k]
            pltpu.make_async_copy(
                y_hbm.at[pl.ds(pl.multiple_of(g * SLOT_GROUP, SLOT_GROUP), SLOT_GROUP)],
                z_scr.at[slot, pl.ds(pl.multiple_of(k * SLOT_GROUP, SLOT_GROUP), SLOT_GROUP)],
                sem.at[slot]).start()
            return carry

        lax.fori_loop(0, gb, issue, 0)

    @pl.when(nbatch > 0)
    def _():
        gather(0, 0)

    acc_scr[...] = jnp.zeros_like(acc_scr)
    t_col = (i * tl + lax.broadcasted_iota(I32, (tl, 1), 0)).astype(F32)
    lane_j = lax.broadcasted_iota(I32, (1, rows_b), 1)

    def batch(bi, carry):
        slot = bi % 2

        @pl.when(bi + 1 < nbatch)
        def _():
            gather(bi + 1, 1 - slot)

        pltpu.make_async_copy(y_hbm.at[pl.ds(0, rows_b)], z_scr.at[slot], sem.at[slot]).wait()
        tok_row = z_scr[slot, :, d:d + LANE].T[0:1, :]
        tok_row = jnp.where(lane_j < (npos - bi * gb) * SLOT_GROUP, tok_row, -1.0)
        onehot = jnp.where(tok_row == t_col, 1.0, 0.0).astype(BF16)
        zh, zl = _split(z_scr[slot, :, :d])
        acc_scr[...] += _bdot(onehot, zh) + _bdot(onehot, zl)
        return carry

    lax.fori_loop(0, nbatch, batch, 0)
    out = x_ref[...] + g_ref[0, 0] * acc_scr[...]
    if nsplit is None:
        o_refs[0][...] = out
    else:
        @pl.when(i < nsplit)
        def _():
            o_refs[0][...] = out

        @pl.when(i >= nsplit)
        def _():
            o_refs[1][...] = out


def moe_combine(x, y, flat, pstart, mods, cidx, k_gate, split=None):
    t, d = x.shape
    tl = TOK_TILE
    blk = pl.BlockSpec((tl, d), lambda i, *_: (i, 0))
    if split is None:
        nsplit, out_specs, out_shape = None, blk, jax.ShapeDtypeStruct((t, d), F32)
    else:
        nsplit, n1 = split // tl, (t - split) // tl
        out_specs = [pl.BlockSpec((tl, d), lambda i, *_: (jnp.minimum(i, nsplit - 1), 0)),
                     pl.BlockSpec((tl, d), lambda i, *_: (jnp.clip(i - nsplit, 0, n1 - 1), 0))]
        out_shape = [jax.ShapeDtypeStruct((split, d), F32), jax.ShapeDtypeStruct((t - split, d), F32)]
    return _call(functools.partial(_combine_kernel, d=d, nsplit=nsplit), grid=(t // tl,), nsp=3,
                 in_specs=[pl.BlockSpec(memory_space=pl.ANY), blk,
                           pl.BlockSpec((1, 1, 1, d), lambda i, f, p, c: (c[i], k_gate, 0, 0))],
                 out_specs=out_specs, out_shape=out_shape,
                 scratch=[pltpu.VMEM((2, COMBINE_BATCH * SLOT_GROUP, y.shape[1]), F32), pltpu.VMEM((tl, d), F32),
                          pltpu.SemaphoreType.DMA((2,))],
                 name="moe_combine")(flat, pstart, cidx, y, x, mods)


def _route(aff_t, groups):
    n_exp = aff_t.shape[0]
    caps = [CAPACITY_FACTOR * seq // n_exp for _, _, seq in groups]
    slots = sum(nb * cap for (_, nb, _), cap in zip(groups, caps))
    e_ids = jnp.arange(n_exp, dtype=I32)[:, None, None]
    rows, gates, g0s, ngs = [], [], [], []
    off = 0
    for (row0, nb, seq), cap in zip(groups, caps):
        a = lax.slice_in_dim(aff_t, row0, row0 + nb * seq, axis=1).reshape(n_exp, nb, seq)
        gate, idx = lax.top_k(a, cap)
        idx, gate = lax.sort((idx.astype(I32), gate), dimension=2, num_keys=1)
        b_ids = jnp.arange(nb, dtype=I32)[None, :, None]
        rows.append((row0 + b_ids * seq + idx).reshape(n_exp, nb * cap))
        gates.append(gate.reshape(n_exp, nb * cap))
        ntile = seq // TOK_TILE
        bounds = jnp.arange(ntile + 1, dtype=I32) * TOK_TILE
        cnt = jnp.sum(idx[:, :, None, :] < bounds[None, None, :, None], axis=-1, dtype=I32)
        base = e_ids * slots + off + b_ids * cap
        lo, hi = base + cnt[..., :-1], base + cnt[..., 1:]
        g0 = lo // SLOT_GROUP
        ng = jnp.where(hi > lo, (hi + SLOT_GROUP - 1) // SLOT_GROUP - g0, 0)
        g0s.append(g0.transpose(1, 2, 0).reshape(nb * ntile, n_exp))
        ngs.append(ng.transpose(1, 2, 0).reshape(nb * ntile, n_exp))
        off += nb * cap
    g0 = jnp.concatenate(g0s, axis=0).reshape(-1)
    ng = jnp.concatenate(ngs, axis=0).reshape(-1)
    ntiles = g0.shape[0] // n_exp
    cum = jnp.cumsum(ng)
    start = cum - ng
    pstart = jnp.concatenate([start[::n_exp], cum[-1:]]).astype(I32)
    pmax = n_exp * slots // SLOT_GROUP + 2 * ntiles * n_exp + COMBINE_BATCH
    p = jnp.arange(pmax, dtype=I32)
    seg = jnp.minimum(jnp.searchsorted(cum, p, side="right", method="compare_all"), g0.shape[0] - 1)
    flat = jnp.where(p < cum[-1], g0[seg] + p - start[seg], 0).astype(I32)
    return jnp.concatenate(rows, axis=1), jnp.concatenate(gates, axis=1), flat, pstart


def ec_moe(x, gain, mods, cidx, router, w1, w3, w2, layer, groups, split=None):
    n_exp = router.shape[1]
    h, aff_t = norm_mod(x, gain, mods, cidx, 3, 4, router_t=router.T)
    rows, gates, flat, pstart = _route(aff_t, groups)
    slots = rows.shape[1]
    tm = _pick(slots, (1024, 512, 256, 128, 64, 32, 16, 8))
    tpe = slots // tm
    rows_f = rows.reshape(-1)
    hm = ffn_up(h, rows_f, w1, w3, layer, tm, tpe)
    lanes = lambda v: jnp.broadcast_to(v.reshape(-1, 1), (n_exp * slots, LANE))
    y = ffn_down(hm, w2, layer, lanes(gates), lanes(rows_f.astype(F32)), tm, tpe)
    return moe_combine(x, y, flat, pstart, mods, cidx, 5, split)


def kernel(x_prompt, x_sample, cache_attn_k, cache_attn_v, state_ret_fwd, state_ret_bwd, c, c_ctx, ada_w, ada_b, norm1_g, norm2_g, ev_w_in, ev_w_out, hy_sconv_w, hy_sconv_b, hy_ffn_w1, hy_ffn_b1, hy_ffn_w2, hy_ffn_b2, hy_ffn_w3, hy_freq, hy_bias, at_q_norm, at_k_norm, at_sink, od_w_in, od_w_out, ret_decay, ret_gn, moe_router, moe_w1, moe_w3, moe_w2):
    bp, lp, d = x_prompt.shape
    bs, ls, _ = x_sample.shape
    depth = ada_w.shape[0]
    tp, ts = bp * lp, bs * ls
    t = tp + ts
    groups = ((0, bp, lp), (tp, bs, ls))
    a_width = hy_bias.shape[-1]
    hd = at_q_norm.shape[-1]
    n_heads = at_sink.shape[-1]
    n_kv = cache_attn_k.shape[3]
    c_heads, c_dv = ret_gn.shape[1], ret_gn.shape[2]
    c_dk = (od_w_in.shape[-1] - 2 * c_heads * c_dv) // (2 * c_heads)

    ntp, nts = tp // TOK_TILE, ts // TOK_TILE
    tps, tss = lp // TOK_TILE, ls // TOK_TILE
    tile = np.arange(ntp + nts)
    cidx = jnp.asarray(np.where(tile < ntp, 0, 1 + (tile - ntp) // tss), I32)
    in_seq = np.where(tile < ntp, tile % tps, (tile - ntp) % tss)
    seq_tiles = np.where(tile < ntp, tps, tss)
    first = jnp.asarray(in_seq == 0, I32)
    last = jnp.asarray(in_seq == seq_tiles - 1, I32)
    tm_mm = _pick(math.gcd(tp, ls), (1024, 512, 256))
    cidx_mm = cidx[::tm_mm // TOK_TILE]

    cond = jnp.concatenate([c_ctx[None, :], c], axis=0)
    cond = jnp.pad(cond, ((0, (-cond.shape[0]) % 8), (0, 0)))
    mods_all = adaln_all(cond, ada_w, ada_b)
    x = (x_prompt.reshape(tp, d), x_sample.reshape(ts, d))

    new_k, new_v, new_sf, new_sb = [], [], [], []
    for l in range(depth):
        mods = mods_all[l].reshape(-1, 6, 1, d)
        j = l // 2
        h = norm_mod(x, norm1_g[l], mods, cidx, 0, 1)
        if l % 2 == 0:
            u = matmul([h], ev_w_in[j], out_dtype=F32, tm=tm_mm)
            z, x0 = hy_pre(u, hy_sconv_w[j], hy_sconv_b[j], first, last, a_width)
            convs = []
            for row0, nb, seq in groups:
                taps = hyena_taps(seq, hy_ffn_w1[j], hy_ffn_b1[j], hy_ffn_w2[j], hy_ffn_b2[j],
                                  hy_ffn_w3[j], hy_freq[j], a_width)
                conv_fn = long_conv_direct if seq <= 512 else long_conv_two_stage
                convs.append(conv_fn(z, row0, nb, seq, taps))
            ya = hy_post(x0, z, hy_bias[j], convs[0], convs[1])
            cos_s, sin_s = _rope_tables(ls, hd)
            cos_t = jnp.concatenate([jnp.ones((tp, hd), F32), jnp.tile(cos_s, (bs, 1))], axis=0)
            sin_t = jnp.concatenate([jnp.zeros((tp, hd), F32), jnp.tile(sin_s, (bs, 1))], axis=0)
            q, k, v, k_normed = qkv_prep(u, 3 * a_width, n_heads, n_kv, hd, at_q_norm[j], at_k_norm[j], cos_t, sin_t)
            yb_p = attn_context(q, k, v, at_sink[j], 0, bp, lp, n_kv, hd)
            past = cache_attn_k.shape[2]
            ck = cache_attn_k[:, j].reshape(bs, past, n_kv * hd)
            cv = cache_attn_v[:, j].reshape(bs, past, n_kv * hd)
            yb_s = attn_latent(q, k, v, ck, cv, at_sink[j], tp, bs, ls, n_kv, hd)
            x = matmul([ya, (yb_p, yb_s)], ev_w_out[j], out_dtype=F32, tm=tm_mm, res=x, mods=mods, cidx=cidx_mm,
                       k_gate=2)
            kv0 = 3 * a_width + n_heads * hd
            new_k.append(k_normed[:tp].reshape(bp, 1, lp, n_kv, hd))
            new_v.append(u[:tp, kv0 + n_kv * hd:kv0 + 2 * n_kv * hd].reshape(bp, 1, lp, n_kv, hd))
        else:
            nqk = c_heads * c_dk
            w_in = od_w_in[j]
            q = matmul([h], w_in, out_dtype=BF16, tm=tm_mm, ncols=nqk)
            kt = matmul_t(h, w_in[:, nqk:2 * nqk].T, out_dtype=BF16, tm=tm_mm)
            vg = matmul([h], w_in, out_dtype=BF16, tm=tm_mm, col0=2 * nqk)
            log_gamma = -jnp.exp(ret_decay[j].astype(F32))
            y_p, sf, sb = retention(q, kt, vg, 0, bp, lp, c_heads, c_dk, c_dv, log_gamma, ret_gn[j], emit_state=True)
            (y_s,) = retention(q, kt, vg, tp, bs, ls, c_heads, c_dk, c_dv, log_gamma, ret_gn[j],
                               s_f0=state_ret_fwd[:, j:j + 1], s_b0=state_ret_bwd[:, j:j + 1])
            x = matmul([(y_p, y_s)], od_w_out[j], out_dtype=F32, tm=tm_mm, tn=256, res=x, mods=mods, cidx=cidx_mm,
                       k_gate=2)
            new_sf.append(sf)
            new_sb.append(sb)
        x = ec_moe(x, norm2_g[l], mods, cidx, moe_router[l], moe_w1, moe_w3, moe_w2, l, groups,
                   split=tp if l == depth - 1 else None)

    cat = lambda parts: jnp.concatenate(parts, axis=1)
    return (x[0].reshape(bp, lp, d), x[1].reshape(bs, ls, d), cat(new_k), cat(new_v), cat(new_sf), cat(new_sb))
```

```python
import functools
import math

import ml_dtypes
import numpy as np
import jax
import jax.numpy as jnp
from jax import lax
from jax.experimental import pallas as pl
from jax.experimental.pallas import tpu as pltpu

F32 = jnp.float32
BF16 = jnp.bfloat16
I32 = jnp.int32
HIGHEST = lax.Precision.HIGHEST

EPS = 1e-6
NEG_INF = -1e30
GRID_W = 64
WINDOW = 128
ATT_BLOCK = 128
RET_CHUNK = 128
ROPE_BASE = 10000.0
A_BANDS = 16
A_SHORT_DECAY_PCT = 0.3
A_LONG_DECAY_PCT = 1.5
A_DECAY_TARGET = 1e-2
CAPACITY_FACTOR = 2

TOK_TILE = 256
LANE = 128
FFT_N2 = 128
VMEM_LIMIT = 56 * 1024 * 1024


def _pick(n, cands):
    for c in cands:
        if n % c == 0:
            return c
    raise ValueError(f"no tile for {n} in {cands}")


def _call(kernel, *, grid, in_specs, out_specs, out_shape, scratch=(), nsp=0, name=None):
    gs = pltpu.PrefetchScalarGridSpec(num_scalar_prefetch=nsp, grid=grid, in_specs=in_specs,
                                      out_specs=out_specs, scratch_shapes=list(scratch))
    cp = pltpu.CompilerParams(dimension_semantics=("arbitrary",) * len(grid), vmem_limit_bytes=VMEM_LIMIT)
    return pl.pallas_call(kernel, grid_spec=gs, out_shape=out_shape, compiler_params=cp, name=name)


def _silu(x):
    return x / (1.0 + jnp.exp(-x))


def _bdot(a, b):
    return jnp.dot(a, b, preferred_element_type=F32)


def _split(x):
    hi = x.astype(BF16)
    lo = (x - hi.astype(F32)).astype(BF16)
    return hi, lo


def _dot3(mh, ml, x):
    xh, xl = _split(x)
    return _bdot(mh, xh) + _bdot(mh, xl) + _bdot(ml, xh)


def _adaln_kernel(c_ref, w_ref, b_ref, o_ref):
    s = _silu(c_ref[...])
    o_ref[0] = _bdot(s.astype(BF16), w_ref[0].astype(BF16)) + b_ref[0]


def adaln_all(cond, ada_w, ada_b):
    depth, d, n6 = ada_w.shape
    rows = cond.shape[0]
    tn = _pick(n6, (1024, 512, 256, 128))
    return _call(
        _adaln_kernel, grid=(depth, n6 // tn),
        in_specs=[pl.BlockSpec((rows, d), lambda l, j: (0, 0)),
                  pl.BlockSpec((1, d, tn), lambda l, j: (l, 0, j)),
                  pl.BlockSpec((1, 1, tn), lambda l, j: (l, 0, j))],
        out_specs=pl.BlockSpec((1, rows, tn), lambda l, j: (l, 0, j)),
        out_shape=jax.ShapeDtypeStruct((depth, rows, n6), F32), name="adaln",
    )(cond, ada_w, ada_b.reshape(depth, 1, n6))


def _row_sources(a, tm):
    if not isinstance(a, tuple):
        return [a], [lambda i: i], None
    a0, a1 = a
    n0, n1 = a0.shape[0] // tm, a1.shape[0] // tm
    return [a0, a1], [lambda i: jnp.minimum(i, n0 - 1), lambda i: jnp.clip(i - n0, 0, n1 - 1)], n0


def _load_rows(refs, i, split):
    if split is None:
        return refs[0][...]
    return jnp.where(i < split, refs[0][...], refs[1][...])


def _norm_mod(x, g_ref, sc_ref, sh_ref):
    ms = jnp.mean(x * x, axis=-1, keepdims=True)
    y = x * lax.rsqrt(ms + EPS) * g_ref[...]
    return y * (1.0 + sc_ref[0, 0]) + sh_ref[0, 0]


def _norm_mod_kernel(cidx_ref, *refs, nsrc, split):
    g_ref, sc_ref, sh_ref, o_ref = refs[nsrc:]
    x = _load_rows(refs[:nsrc], pl.program_id(0), split)
    o_ref[...] = _norm_mod(x, g_ref, sc_ref, sh_ref).astype(o_ref.dtype)


def _norm_router_kernel(cidx_ref, *refs, nsrc, split):
    g_ref, sc_ref, sh_ref, rt_ref, h_ref, aff_ref = refs[nsrc:]
    x = _load_rows(refs[:nsrc], pl.program_id(0), split)
    h = _norm_mod(x, g_ref, sc_ref, sh_ref)
    h_ref[...] = h
    logits = lax.dot_general(rt_ref[...], h, (((1,), (1,)), ((), ())), precision=HIGHEST,
                             preferred_element_type=F32)
    m = jnp.max(logits, axis=0, keepdims=True)
    e = jnp.exp(logits - m)
    aff_ref[...] = e / jnp.sum(e, axis=0, keepdims=True)


def norm_mod(x, gain, mods, cidx, k_shift, k_scale, *, router_t=None):
    tm = TOK_TILE
    srcs, fns, split = _row_sources(x, tm)
    t = sum(a.shape[0] for a in srcs)
    d = srcs[0].shape[1]
    specs = [pl.BlockSpec((tm, d), lambda i, c, f=f: (f(i), 0)) for f in fns]
    specs += [pl.BlockSpec((1, d), lambda i, c: (0, 0)),
              pl.BlockSpec((1, 1, 1, d), lambda i, c: (c[i], k_scale, 0, 0)),
              pl.BlockSpec((1, 1, 1, d), lambda i, c: (c[i], k_shift, 0, 0))]
    kw = dict(nsrc=len(srcs), split=split)
    if router_t is None:
        return _call(functools.partial(_norm_mod_kernel, **kw), grid=(t // tm,), nsp=1, in_specs=specs,
                     out_specs=pl.BlockSpec((tm, d), lambda i, c: (i, 0)),
                     out_shape=jax.ShapeDtypeStruct((t, d), BF16), name="norm_mod",
                     )(cidx, *srcs, gain.reshape(1, d), mods, mods)
    e = router_t.shape[0]
    specs.append(pl.BlockSpec((e, d), lambda i, c: (0, 0)))
    return _call(functools.partial(_norm_router_kernel, **kw), grid=(t // tm,), nsp=1, in_specs=specs,
                 out_specs=[pl.BlockSpec((tm, d), lambda i, c: (i, 0)),
                            pl.BlockSpec((e, tm), lambda i, c: (0, i))],
                 out_shape=[jax.ShapeDtypeStruct((t, d), F32), jax.ShapeDtypeStruct((e, t), F32)],
                 name="norm_router",
                 )(cidx, *srcs, gain.reshape(1, d), mods, mods, router_t)


def _mm_kernel(*refs, xcounts, xsplits, gated, rcount, rsplit):
    pos = 1 if gated else 0
    i = pl.program_id(1)
    x_refs = []
    for cnt in xcounts:
        x_refs.append(refs[pos:pos + cnt])
        pos += cnt
    nx = len(xcounts)
    w_refs = refs[pos:pos + nx]
    pos += nx
    if gated:
        res_refs, gate_ref = refs[pos:pos + rcount], refs[pos + rcount]
        pos += rcount + 1
    o_ref = refs[pos]
    wb_refs = refs[pos + 1:]

    @pl.when(i == 0)
    def _():
        for w_ref, wb in zip(w_refs, wb_refs):
            wb[...] = w_ref[...].astype(BF16)

    acc = None
    for xr, split, wb in zip(x_refs, xsplits, wb_refs):
        part = _bdot(_load_rows(xr, i, split), wb[...])
        acc = part if acc is None else acc + part
    if gated:
        acc = _load_rows(res_refs, i, rsplit) + gate_ref[0, 0] * acc
    o_ref[...] = acc.astype(o_ref.dtype)


def matmul(xs, w, *, out_dtype, tm, tn=512, col0=0, ncols=None, res=None, mods=None, cidx=None, k_gate=None):
    srcs = [_row_sources(x, tm) for x in xs]
    m = sum(a.shape[0] for a in srcs[0][0])
    kk = srcs[0][0][0].shape[1]
    nx = len(xs)
    assert w.shape[0] == nx * kk
    n = w.shape[1] - col0 if ncols is None else ncols
    tn = _pick(math.gcd(n, col0) if col0 else n, (tn, 512, 256, 128))
    cb0 = col0 // tn
    gated = res is not None
    if gated:
        imap = lambda f: (lambda j, i, c: f(j, i, c))
    else:
        imap = lambda f: (lambda j, i: f(j, i, None))
    specs, args = [], []
    for arrs, fns, _ in srcs:
        specs += [pl.BlockSpec((tm, kk), imap(lambda j, i, c, f=f: (f(i), 0))) for f in fns]
        args += arrs
    specs += [pl.BlockSpec((kk, tn), imap(lambda j, i, c, q=q: (q, cb0 + j))) for q in range(nx)]
    args += [w] * nx
    rcount, rsplit = 0, None
    if gated:
        assert mods.shape[-1] == n and col0 == 0
        rarrs, rfns, rsplit = _row_sources(res, tm)
        rcount = len(rarrs)
        specs += [pl.BlockSpec((tm, tn), imap(lambda j, i, c, f=f: (f(i), j))) for f in rfns]
        stride = tm // TOK_TILE
        specs.append(pl.BlockSpec((1, 1, 1, tn), imap(lambda j, i, c: (c[i * stride], k_gate, 0, j))))
        args = [cidx] + args + rarrs + [mods]
    kern = functools.partial(_mm_kernel, xcounts=tuple(len(s[0]) for s in srcs),
                             xsplits=tuple(s[2] for s in srcs), gated=gated, rcount=rcount, rsplit=rsplit)
    return _call(kern, grid=(n // tn, m // tm), nsp=1 if gated else 0, in_specs=specs,
                 out_specs=pl.BlockSpec((tm, tn), imap(lambda j, i, c: (i, j))),
                 out_shape=jax.ShapeDtypeStruct((m, n), out_dtype),
                 scratch=[pltpu.VMEM((kk, tn), BF16) for _ in xs], name="matmul",
                 )(*args)


def _mm_t_kernel(x_ref, w_ref, o_ref, wb):
    @pl.when(pl.program_id(1) == 0)
    def _():
        wb[...] = w_ref[...].astype(BF16)

    o_ref[...] = lax.dot_general(wb[...], x_ref[...], (((1,), (1,)), ((), ())),
                                 preferred_element_type=F32).astype(o_ref.dtype)


def matmul_t(x, wt, *, out_dtype, tm, tn=512):
    m, kk = x.shape
    n = wt.shape[0]
    tn = _pick(n, (tn, 512, 256, 128))
    return _call(_mm_t_kernel, grid=(n // tn, m // tm),
                 in_specs=[pl.BlockSpec((tm, kk), lambda j, i: (i, 0)), pl.BlockSpec((tn, kk), lambda j, i: (j, 0))],
                 out_specs=pl.BlockSpec((tn, tm), lambda j, i: (j, i)),
                 out_shape=jax.ShapeDtypeStruct((n, m), out_dtype),
                 scratch=[pltpu.VMEM((tn, kk), BF16)], name="matmul_t")(x, wt)


def _hy_pre_kernel(first_ref, last_ref, *refs):
    ins = refs[:15]
    z_ref, x0_ref = refs[15], refs[16]
    i = pl.program_id(0)
    is_first = first_ref[i] == 1
    is_last = last_ref[i] == 1

    def sconv(m_ref, p_ref, n_ref, w_ref, b_ref):
        x = m_ref[...]
        tl = x.shape[0]
        rows = lax.broadcasted_iota(I32, x.shape, 0)
        prev_row = jnp.where(is_first, 0.0, p_ref[7:8, :])
        next_row = jnp.where(is_last, 0.0, n_ref[0:1, :])
        xm1 = jnp.where(rows == 0, prev_row, pltpu.roll(x, 1, 0))
        xp1 = jnp.where(rows == tl - 1, next_row, pltpu.roll(x, tl - 1, 0))
        w = w_ref[...]
        return xm1 * w[0:1] + x * w[1:2] + xp1 * w[2:3] + b_ref[...]

    hv = sconv(ins[0], ins[1], ins[2], ins[9], ins[12])
    x0 = sconv(ins[3], ins[4], ins[5], ins[10], ins[13])
    x1 = sconv(ins[6], ins[7], ins[8], ins[11], ins[14])
    z_ref[...] = x1 * hv
    x0_ref[...] = x0


def hy_pre(u, sc_w, sc_b, first, last, a_width):
    t = u.shape[0]
    tl = TOK_TILE
    cb = _pick(a_width, (512, 256, 128))
    ncb = a_width // cb
    nt8 = t // 8
    specs = []
    for part in range(3):
        off = part * ncb
        specs += [pl.BlockSpec((tl, cb), lambda i, j, f, l, off=off: (i, off + j)),
                  pl.BlockSpec((8, cb), lambda i, j, f, l, off=off: (jnp.maximum(i * (tl // 8) - 1, 0), off + j)),
                  pl.BlockSpec((8, cb), lambda i, j, f, l, off=off: (jnp.minimum((i + 1) * (tl // 8), nt8 - 1), off + j))]
    for part in range(3):
        off = part * ncb
        specs.append(pl.BlockSpec((3, cb), lambda i, j, f, l, off=off: (0, off + j)))
    for part in range(3):
        off = part * ncb
        specs.append(pl.BlockSpec((1, cb), lambda i, j, f, l, off=off: (0, off + j)))
    out_spec = pl.BlockSpec((tl, cb), lambda i, j, f, l: (i, j))
    scb = sc_b.reshape(1, -1)
    return _call(_hy_pre_kernel, grid=(t // tl, ncb), nsp=2, in_specs=specs,
                 out_specs=[out_spec, out_spec],
                 out_shape=[jax.ShapeDtypeStruct((t, a_width), F32)] * 2, name="hy_pre",
                 )(first, last, *([u] * 9), sc_w, sc_w, sc_w, scb, scb, scb)


def _filter_kernel(fa_ref, fb_ref, w1_ref, b1_ref, w2_ref, b2_ref, fr_ref, w3f_ref, w3b_ref, dl_ref, o_ref):
    hdot = functools.partial(jnp.dot, precision=HIGHEST, preferred_element_type=F32)

    def taps(feats, w3_ref):
        h = jnp.sin(fr_ref[0:1, :] * (hdot(feats, w1_ref[...]) + b1_ref[...]))
        h = jnp.sin(fr_ref[1:2, :] * (hdot(h, w2_ref[...]) + b2_ref[...]))
        return hdot(h, w3_ref[...]) * jnp.exp(-feats[:, 0:1] * dl_ref[...])

    hf = taps(fa_ref[...], w3f_ref)
    hb = taps(fb_ref[...], w3b_ref)
    s = (jnp.sum(jnp.abs(hf), axis=0, keepdims=True) + jnp.sum(jnp.abs(hb), axis=0, keepdims=True)) + EPS
    o_ref[0] = hf / s
    o_ref[1] = jnp.where(lax.broadcasted_iota(I32, hb.shape, 0) == 0, 0.0, hb / s)


def hyena_taps(seq, w1, b1, w2, b2, w3, freq, a_width):
    t = jnp.arange(seq, dtype=F32) / seq
    bands = jnp.linspace(1e-4, A_BANDS - 1, A_BANDS, dtype=F32)
    ang = 2.0 * math.pi * t[:, None] * bands[None, :]
    feats = jnp.concatenate([t[:, None], jnp.cos(ang), -jnp.sin(ang)], axis=-1)
    emb = feats.shape[1]
    feats = jnp.pad(feats, ((0, 0), (0, LANE - emb)))
    feats_b = jnp.roll(feats[::-1], 1, axis=0)
    w1p = jnp.pad(w1, ((0, LANE - emb), (0, 0)))
    max_decay = math.log(A_DECAY_TARGET) / A_SHORT_DECAY_PCT
    min_decay = math.log(A_DECAY_TARGET) / A_LONG_DECAY_PCT
    deltas = jnp.abs(jnp.linspace(min_decay, max_decay, a_width, dtype=F32)).reshape(1, a_width)
    ffn = w2.shape[0]
    cb = _pick(a_width, (256, 128))
    ncb = a_width // cb
    full = lambda shape: pl.BlockSpec(shape, lambda j: (0,) * len(shape))
    taps = _call(_filter_kernel, grid=(ncb,),
                 in_specs=[full((seq, LANE)), full((seq, LANE)), full((LANE, ffn)), full((1, ffn)), full((ffn, ffn)),
                           full((1, ffn)), full((2, ffn)),
                           pl.BlockSpec((ffn, cb), lambda j: (0, j)),
                           pl.BlockSpec((ffn, cb), lambda j: (0, ncb + j)),
                           pl.BlockSpec((1, cb), lambda j: (0, j))],
                 out_specs=pl.BlockSpec((2, seq, cb), lambda j: (0, 0, j)),
                 out_shape=jax.ShapeDtypeStruct((2, seq, a_width), F32), name="hy_filter",
                 )(feats, feats_b, w1p, b1.reshape(1, ffn), w2, b2.reshape(1, ffn), freq, w3, w3, deltas)
    return taps.reshape(2 * seq, a_width)


def _hilo(x):
    x32 = np.asarray(x, np.float32)
    hi = x32.astype(ml_dtypes.bfloat16)
    lo = (x32 - hi.astype(np.float32)).astype(ml_dtypes.bfloat16)
    return hi, lo


@functools.lru_cache(maxsize=None)
def _dft_direct_consts(seq):
    n = 2 * seq
    k = np.arange(n)
    ang = 2.0 * np.pi * ((k[:, None] * k[None, :]) % n) / n
    c, s = np.cos(ang), np.sin(ang)
    fwd = np.block([[c[:, :seq], s[:, :seq]], [-s[:, :seq], c[:, :seq]]])
    taps = np.concatenate([c, -s], axis=0)
    inv = np.block([[c[:seq, :], -s[:seq, :]], [s[:seq, :], c[:seq, :]]]) / n
    return _hilo(fwd), _hilo(taps), _hilo(inv)


@functools.lru_cache(maxsize=None)
def _dft_two_stage_consts(seq):
    n = 2 * seq
    n2c = FFT_N2
    n1c = n // n2c
    h = n1c // 2
    n2 = np.arange(n2c)[:, None, None]
    k1 = np.arange(n1c)[None, :, None]
    n1 = np.arange(n1c)[None, None, :]
    psi = 2.0 * np.pi * ((n1 * k1 * n2c + n2 * k1) % n) / n
    c, s = np.cos(psi), np.sin(psi)
    m1d = np.concatenate([np.concatenate([c[:, :, :h], s[:, :, :h]], axis=2),
                          np.concatenate([-s[:, :, :h], c[:, :, :h]], axis=2)], axis=1)
    m1t = np.concatenate([c, -s], axis=1)
    ct = np.swapaxes(c, 1, 2)[:, :h, :]
    st = np.swapaxes(s, 1, 2)[:, :h, :]
    m1i = np.concatenate([np.concatenate([ct, -st], axis=2),
                          np.concatenate([st, ct], axis=2)], axis=1) / n
    kk = np.arange(n2c)
    th = 2.0 * np.pi * ((kk[:, None] * kk[None, :]) % n2c) / n2c
    c2, s2 = np.cos(th), np.sin(th)
    m2 = np.block([[c2, s2], [-s2, c2]])
    m2i = np.block([[c2, -s2], [s2, c2]])
    return _hilo(m1d), _hilo(m1t), _hilo(m1i), _hilo(m2), _hilo(m2i)


def _cmul_stacked(x, hs, half):
    xr, xi = x[:half], x[half:]
    hr, hi = hs[:half], hs[half:]
    return jnp.concatenate([xr * hr - xi * hi, xr * hi + xi * hr], axis=0)


def _const_mm3_kernel(mh_ref, ml_ref, x_ref, o_ref):
    o_ref[...] = _dot3(mh_ref[...], ml_ref[...], x_ref[...])


def const_mm3(mh, ml, x):
    m, k = mh.shape
    c = x.shape[1]
    cb = _pick(c, (256, 128))
    return _call(_const_mm3_kernel, grid=(c // cb,),
                 in_specs=[pl.BlockSpec((m, k), lambda j: (0, 0)), pl.BlockSpec((m, k), lambda j: (0, 0)),
                           pl.BlockSpec((k, cb), lambda j: (0, j))],
                 out_specs=pl.BlockSpec((m, cb), lambda j: (0, j)),
                 out_shape=jax.ShapeDtypeStruct((m, c), F32), name="dft_taps")(mh, ml, x)


def _conv_direct_kernel(fh_ref, fl_ref, gh_ref, gl_ref, zr_ref, zi_ref, h_ref, yr_ref, yi_ref, *, seq):
    x = jnp.concatenate([zr_ref[...], zi_ref[...]], axis=0)
    spec = _dot3(fh_ref[...], fl_ref[...], x)
    y = _dot3(gh_ref[...], gl_ref[...], _cmul_stacked(spec, h_ref[...], 2 * seq))
    yr_ref[...] = y[:seq]
    yi_ref[...] = y[seq:]


def long_conv_direct(z, row0, nb, seq, taps):
    a_width = z.shape[1]
    (fh, fl), (th, tl_), (gh, gl) = _dft_direct_consts(seq)
    spec_h = const_mm3(jnp.asarray(th), jnp.asarray(tl_), taps)
    cb = _pick(a_width, (256, 128))
    half = nb // 2
    b0 = row0 // seq
    full = lambda a: pl.BlockSpec(a.shape, lambda j, p: (0, 0))
    yr, yi = _call(
        functools.partial(_conv_direct_kernel, seq=seq), grid=(a_width // cb, half),
        in_specs=[full(fh), full(fl), full(gh), full(gl),
                  pl.BlockSpec((seq, cb), lambda j, p: (b0 + p, j)),
                  pl.BlockSpec((seq, cb), lambda j, p: (b0 + half + p, j)),
                  pl.BlockSpec((4 * seq, cb), lambda j, p: (0, j))],
        out_specs=[pl.BlockSpec((seq, cb), lambda j, p: (p, j))] * 2,
        out_shape=[jax.ShapeDtypeStruct((half * seq, a_width), F32)] * 2, name="conv_direct",
    )(jnp.asarray(fh), jnp.asarray(fl), jnp.asarray(gh), jnp.asarray(gl), z, z, spec_h)
    return jnp.concatenate([yr, yi], axis=0)


def _fft_s1_kernel(mh_ref, ml_ref, *refs, g, packed):
    o_ref = refs[-1]
    for j in range(g):
        if packed:
            x = jnp.concatenate([refs[0][0, j], refs[1][0, j]], axis=0)
        else:
            x = refs[0][0, j]
        o_ref[0, j] = _dot3(mh_ref[j], ml_ref[j], x)


def _fft_s1(mh, ml, zp, *, packed):
    n2c, m_rows, n1c = mh.shape
    bsz, _, rows, c = zp.shape
    p = bsz // 2 if packed else bsz
    cb = _pick(c, (256, 128))
    g = 16
    cspec = pl.BlockSpec((g, m_rows, n1c), lambda q, j, t: (t, 0, 0))
    zspecs = [pl.BlockSpec((1, g, rows, cb), lambda q, j, t: (q, t, 0, j))]
    args = [zp]
    if packed:
        zspecs.append(pl.BlockSpec((1, g, rows, cb), lambda q, j, t: (q + p, t, 0, j)))
        args.append(zp)
    return _call(functools.partial(_fft_s1_kernel, g=g, packed=packed), grid=(p, c // cb, n2c // g),
                 in_specs=[cspec, cspec] + zspecs,
                 out_specs=pl.BlockSpec((1, g, m_rows, cb), lambda q, j, t: (q, t, 0, j)),
                 out_shape=jax.ShapeDtypeStruct((p, n2c, m_rows, c), F32), name="fft_s1",
                 )(jnp.asarray(mh), jnp.asarray(ml), *args)


def _fft_s2_kernel(mh_ref, ml_ref, ih_ref, il_ref, *refs, g, conv):
    a_ref, o_ref = refs[0], refs[-1]
    half = a_ref.shape[2] // 2
    for j in range(g):
        spec = _dot3(mh_ref[...], ml_ref[...], a_ref[0, j])
        if conv:
            spec = _dot3(ih_ref[...], il_ref[...], _cmul_stacked(spec, refs[1][0, j], half))
        o_ref[0, j] = spec


def _fft_s2(m2, m2i, at, spec_h=None):
    p, n1c, rows, c = at.shape
    cb = _pick(c, (256, 128))
    g = _pick(n1c, (8, 4, 2, 1))
    conv = spec_h is not None
    mspec = pl.BlockSpec((rows, rows), lambda q, j, t: (0, 0))
    blk = pl.BlockSpec((1, g, rows, cb), lambda q, j, t: (q, t, 0, j))
    specs = [mspec] * 4 + [blk]
    args = [jnp.asarray(m2[0]), jnp.asarray(m2[1]), jnp.asarray(m2i[0]), jnp.asarray(m2i[1]), at]
    if conv:
        specs.append(pl.BlockSpec((1, g, rows, cb), lambda q, j, t: (0, t, 0, j)))
        args.append(spec_h)
    return _call(functools.partial(_fft_s2_kernel, g=g, conv=conv), grid=(p, c // cb, n1c // g),
                 in_specs=specs, out_specs=blk, out_shape=jax.ShapeDtypeStruct(at.shape, F32), name="fft_s2",
                 )(*args)


def _fft_s3_kernel(mh_ref, ml_ref, b_ref, yr_ref, yi_ref, *, g):
    half = yr_ref.shape[2]
    for j in range(g):
        y = _dot3(mh_ref[j], ml_ref[j], b_ref[0, j])
        yr_ref[0, j] = y[:half]
        yi_ref[0, j] = y[half:]


def _fft_s3(mh, ml, bt):
    n2c, n1c, rows = mh.shape
    p, _, _, c = bt.shape
    cb = _pick(c, (256, 128))
    g = 16
    h = n1c // 2
    oblk = pl.BlockSpec((1, g, h, cb), lambda q, j, t: (q, t, 0, j))
    cspec = pl.BlockSpec((g, n1c, rows), lambda q, j, t: (t, 0, 0))
    return _call(functools.partial(_fft_s3_kernel, g=g), grid=(p, c // cb, n2c // g),
                 in_specs=[cspec, cspec, pl.BlockSpec((1, g, rows, cb), lambda q, j, t: (q, t, 0, j))],
                 out_specs=[oblk, oblk],
                 out_shape=[jax.ShapeDtypeStruct((p, n2c, h, c), F32)] * 2, name="fft_s3",
                 )(jnp.asarray(mh), jnp.asarray(ml), bt)


def _swap_digits(a):
    p, x, y2, c = a.shape
    y = y2 // 2
    return a.reshape(p, x, 2, y, c).transpose(0, 3, 2, 1, 4).reshape(p, y, 2 * x, c)


def long_conv_two_stage(z, row0, nb, seq, taps):
    a_width = z.shape[1]
    m1d, m1t, m1i, m2, m2i = _dft_two_stage_consts(seq)
    n2c = FFT_N2
    n1c = 2 * seq // n2c
    taps_p = taps.reshape(1, n1c, n2c, a_width).transpose(0, 2, 1, 3)
    spec_h = _fft_s2(m2, m2i, _swap_digits(_fft_s1(m1t[0], m1t[1], taps_p, packed=False)))
    zs = lax.slice_in_dim(z, row0, row0 + nb * seq, axis=0)
    zp = zs.reshape(nb, n1c // 2, n2c, a_width).transpose(0, 2, 1, 3)
    a = _swap_digits(_fft_s1(m1d[0], m1d[1], zp, packed=True))
    b = _swap_digits(_fft_s2(m2, m2i, a, spec_h))
    yr, yi = _fft_s3(m1i[0], m1i[1], b)
    y = jnp.concatenate([yr, yi], axis=0)
    return y.transpose(0, 2, 1, 3).reshape(nb * seq, a_width)


def _hy_post_kernel(x0_ref, z_ref, bias_ref, cp_ref, cs_ref, o_ref, *, ntp):
    conv = jnp.where(pl.program_id(0) < ntp, cp_ref[...], cs_ref[...])
    o_ref[...] = (x0_ref[...] * (conv + z_ref[...] * bias_ref[...])).astype(o_ref.dtype)


def hy_post(x0, z, bias, conv_p, conv_s):
    t, a_width = x0.shape
    tl = TOK_TILE
    ntp = conv_p.shape[0] // tl
    nts = conv_s.shape[0] // tl
    cb = _pick(a_width, (512, 256, 128))
    blk = pl.BlockSpec((tl, cb), lambda i, j: (i, j))
    return _call(functools.partial(_hy_post_kernel, ntp=ntp), grid=(t // tl, a_width // cb),
                 in_specs=[blk, blk, pl.BlockSpec((1, cb), lambda i, j: (0, j)),
                           pl.BlockSpec((tl, cb), lambda i, j: (jnp.minimum(i, ntp - 1), j)),
                           pl.BlockSpec((tl, cb), lambda i, j: (jnp.clip(i - ntp, 0, nts - 1), j))],
                 out_specs=blk, out_shape=jax.ShapeDtypeStruct((t, a_width), BF16), name="hy_post",
                 )(x0, z, bias.reshape(1, a_width), conv_p, conv_s)


def _qkv_prep_kernel(q_ref, k_ref, v_ref, qn_ref, kn_ref, cos_ref, sin_ref, qo_ref, ko_ref, vo_ref, kc_ref, *, hd, scale):
    cos = cos_ref[...]
    sin = sin_ref[...]
    lane = lax.broadcasted_iota(I32, cos.shape, 1)
    low = (lane % (hd // 2)) < (hd // 4)

    def norm(x, g):
        return x * lax.rsqrt(jnp.mean(x * x, axis=-1, keepdims=True) + EPS) * g

    def rope(x):
        partner = jnp.where(low, pltpu.roll(x, hd - hd // 4, 1), pltpu.roll(x, hd // 4, 1))
        return x * cos + partner * sin

    for h in range(q_ref.shape[1] // hd):
        sl = slice(h * hd, (h + 1) * hd)
        qo_ref[:, sl] = (rope(norm(q_ref[:, sl], qn_ref[...])) * scale).astype(qo_ref.dtype)
    for h in range(k_ref.shape[1] // hd):
        sl = slice(h * hd, (h + 1) * hd)
        kn = norm(k_ref[:, sl], kn_ref[...])
        kc_ref[:, sl] = kn
        ko_ref[:, sl] = rope(kn).astype(ko_ref.dtype)
    vo_ref[...] = v_ref[...].astype(vo_ref.dtype)


def qkv_prep(u, col0, n_heads, n_kv, hd, qn, kn, cos_t, sin_t):
    t = u.shape[0]
    tl = TOK_TILE
    qw, kw = n_heads * hd, n_kv * hd
    assert col0 % qw == 0 and (col0 + qw) % kw == 0
    qb, kb = col0 // qw, (col0 + qw) // kw
    row = lambda w: pl.BlockSpec((tl, w), lambda i: (i, 0))
    vec = pl.BlockSpec((1, hd), lambda i: (0, 0))
    return _call(functools.partial(_qkv_prep_kernel, hd=hd, scale=hd ** -0.5), grid=(t // tl,),
                 in_specs=[pl.BlockSpec((tl, qw), lambda i: (i, qb)),
                           pl.BlockSpec((tl, kw), lambda i: (i, kb)),
                           pl.BlockSpec((tl, kw), lambda i: (i, kb + 1)),
                           vec, vec, row(hd), row(hd)],
                 out_specs=[row(qw), row(kw), row(kw), row(kw)],
                 out_shape=[jax.ShapeDtypeStruct((t, qw), BF16), jax.ShapeDtypeStruct((t, kw), BF16),
                            jax.ShapeDtypeStruct((t, kw), BF16), jax.ShapeDtypeStruct((t, kw), F32)],
                 name="qkv_prep")(u, u, u, qn.reshape(1, hd), kn.reshape(1, hd), cos_t, sin_t)


def _rope_tables(seq, hd):
    rows = seq // GRID_W
    row = jnp.repeat(jnp.arange(rows), GRID_W).astype(F32)
    col = jnp.tile(jnp.arange(GRID_W), rows).astype(F32)
    nf = hd // 4
    inv = jnp.exp(-math.log(ROPE_BASE) * jnp.arange(nf, dtype=F32) / nf)
    ang_r = row[:, None] * inv[None, :]
    ang_c = col[:, None] * inv[None, :]
    cos = jnp.concatenate([jnp.cos(ang_r)] * 2 + [jnp.cos(ang_c)] * 2, axis=-1)
    sin = jnp.concatenate([-jnp.sin(ang_r), jnp.sin(ang_r), -jnp.sin(ang_c), jnp.sin(ang_c)], axis=-1)
    return cos, sin


def _attn_kernel(sink_ref, q_ref, *refs, kinds, group, hd, nblk):
    nk = len(kinds)
    k_refs, v_refs, o_ref = refs[:nk], refs[nk:2 * nk], refs[2 * nk]
    kvg = pl.program_id(1)
    i = pl.program_id(2)
    tq = q_ref.shape[0]
    q = jnp.concatenate([q_ref[:, g * hd:(g + 1) * hd] for g in range(group)], axis=0)
    rows = lax.broadcasted_iota(I32, (group * tq, 1), 0)
    sink = jnp.full((group * tq, 1), sink_ref[kvg * group], F32)
    for g in range(1, group):
        sink = jnp.where(rows >= g * tq, sink_ref[kvg * group + g], sink)

    def load(ref):
        x = ref[0] if len(ref.shape) == 3 else ref[...]
        return x.astype(BF16)

    scores = []
    m = sink
    for kind, k_ref in zip(kinds, k_refs):
        s = lax.dot_general(q, load(k_ref), (((1,), (1,)), ((), ())), preferred_element_type=F32)
        if kind in ("prev", "next"):
            r = lax.broadcasted_iota(I32, s.shape, 0) % tq
            c = lax.broadcasted_iota(I32, s.shape, 1)
            if kind == "prev":
                ok = c >= r + jnp.where(i > 0, 0, 2 * tq)
            else:
                ok = c <= r - jnp.where(i < nblk - 1, 0, 2 * tq)
            s = jnp.where(ok, s, NEG_INF)
        scores.append(s)
        m = jnp.maximum(m, jnp.max(s, axis=-1, keepdims=True))
    den = jnp.exp(sink - m)
    acc = None
    for s, v_ref in zip(scores, v_refs):
        p = jnp.exp(s - m)
        den = den + jnp.sum(p, axis=-1, keepdims=True)
        pv = _bdot(p.astype(BF16), load(v_ref))
        acc = pv if acc is None else acc + pv
    out = acc / den
    o_ref[...] = jnp.concatenate([out[g * tq:(g + 1) * tq] for g in range(group)], axis=1).astype(o_ref.dtype)


def attn_context(q, k, v, sink, row0, nb, seq, n_kv, hd):
    group = q.shape[1] // (n_kv * hd)
    b0 = row0 // seq
    kv = pl.BlockSpec((seq, hd), lambda b, g, i, s: (b0 + b, g))
    return _call(functools.partial(_attn_kernel, kinds=("all",), group=group, hd=hd, nblk=1),
                 grid=(nb, n_kv, 1), nsp=1,
                 in_specs=[pl.BlockSpec((seq, group * hd), lambda b, g, i, s: (b0 + b, g)), kv, kv],
                 out_specs=pl.BlockSpec((seq, group * hd), lambda b, g, i, s: (b, g)),
                 out_shape=jax.ShapeDtypeStruct((nb * seq, q.shape[1]), BF16), name="attn_context",
                 )(sink, q, k, v)


def attn_latent(q, k, v, ck, cv, sink, row0, nb, seq, n_kv, hd):
    group = q.shape[1] // (n_kv * hd)
    blk = ATT_BLOCK
    nblk = seq // blk
    r0 = row0 // blk
    past = ck.shape[1]
    cur = lambda b, g, i, s: (r0 + b * nblk + i, g)
    prev = lambda b, g, i, s: (r0 + b * nblk + jnp.maximum(i - 1, 0), g)
    nxt = lambda b, g, i, s: (r0 + b * nblk + jnp.minimum(i + 1, nblk - 1), g)
    ctx = pl.BlockSpec((1, past, hd), lambda b, g, i, s: (b, 0, g))
    kvs = [pl.BlockSpec((blk, hd), f) for f in (prev, cur, nxt)]
    return _call(functools.partial(_attn_kernel, kinds=("prev", "cur", "next", "ctx"), group=group, hd=hd, nblk=nblk),
                 grid=(nb, n_kv, nblk), nsp=1,
                 in_specs=[pl.BlockSpec((blk, group * hd), cur)] + kvs + [ctx] + kvs + [ctx],
                 out_specs=pl.BlockSpec((blk, group * hd), lambda b, g, i, s: (b * nblk + i, g)),
                 out_shape=jax.ShapeDtypeStruct((nb * seq, q.shape[1]), BF16), name="attn_latent",
                 )(sink, q, k, k, k, ck, v, v, v, cv)


def _ret_kernel(lg_ref, q_ref, kt_ref, v_ref, g_ref, gn_ref, *refs, ch, nc, dk, has_init, emit_state):
    pos = 0
    if has_init:
        s0_refs = refs[0:2]
        pos = 2
    y_ref = refs[pos]
    pos += 1
    if emit_state:
        so_refs = refs[pos:pos + 2]
        pos += 2
    o_scr, s_scr = refs[pos], refs[pos + 1]
    h = pl.program_id(1)
    kscale = dk ** -0.5
    ii = lax.broadcasted_iota(I32, (ch, ch), 0).astype(F32)
    jj = lax.broadcasted_iota(I32, (ch, ch), 1).astype(F32)
    col = lax.broadcasted_iota(I32, (ch, 1), 0).astype(F32)
    lane = lax.broadcasted_iota(I32, (1, ch), 1).astype(F32)
    gn = gn_ref[0]

    consts = []
    for d in range(2):
        lg = lg_ref[d, h]
        diff = ii - jj if d == 0 else jj - ii
        dmat = jnp.where(diff >= 0, jnp.exp(lg * jnp.maximum(diff, 0.0)), 0.0) * kscale
        xi = jnp.exp(lg * (col + 1.0)) if d == 0 else jnp.exp(lg * (ch - col))
        zeta = (jnp.exp(lg * (ch - 1.0 - lane)) if d == 0 else jnp.exp(lg * lane)) * kscale
        cdec = jnp.exp(lg * jnp.full((1, 1), float(ch), F32))
        consts.append((dmat, xi, zeta, cdec))
        if has_init:
            s_scr[d] = s0_refs[d][0, 0, 0]
        else:
            s_scr[d] = jnp.zeros(s_scr.shape[1:], F32)

    def chunk_out(d, c):
        dmat, xi, zeta, cdec = consts[d]
        r0 = pl.multiple_of(c * ch, ch)
        qc = q_ref[0, pl.ds(r0, ch), :]
        ktc = kt_ref[:, pl.ds(r0, ch)]
        vc = v_ref[0, pl.ds(r0, ch), :]
        sc = _bdot(qc, ktc) * dmat
        s = s_scr[d]
        o = _bdot(sc.astype(BF16), vc) + _bdot(qc, s.astype(BF16)) * xi
        kz = (ktc.astype(F32) * zeta).astype(BF16)
        s_scr[d] = s * cdec + _bdot(kz, vc)
        return r0, o

    def finalize(r0, o):
        mu = jnp.mean(o, axis=-1, keepdims=True)
        var = jnp.mean(jnp.square(o - mu), axis=-1, keepdims=True)
        on = (o - mu) * lax.rsqrt(var + EPS) * gn
        gate = _silu(g_ref[0, pl.ds(r0, ch), :].astype(F32))
        y_ref[0, pl.ds(r0, ch), :] = (gate * on).astype(y_ref.dtype)

    def first_half(n, carry):
        for d in range(2):
            r0, o = chunk_out(d, n if d == 0 else nc - 1 - n)
            o_scr[pl.ds(r0, ch), :] = o
        return carry

    def second_half(n, carry):
        for d in range(2):
            r0, o = chunk_out(d, n if d == 0 else nc - 1 - n)
            finalize(r0, o_scr[pl.ds(r0, ch), :] + o)
        return carry

    lax.fori_loop(0, nc // 2, first_half, 0)
    lax.fori_loop(nc // 2, nc, second_half, 0)
    if emit_state:
        for d in range(2):
            so_refs[d][0, 0, 0] = s_scr[d]


def retention(q, kt, vg, row0, nb, seq, n_heads, dk, dv, log_gamma, gn, *, s_f0=None, s_b0=None, emit_state=False):
    t = q.shape[0]
    q3 = q.reshape(t // seq, seq, n_heads * dk)
    vg3 = vg.reshape(t // seq, seq, 2 * n_heads * dv)
    b0 = row0 // seq
    has_init = s_f0 is not None
    specs = [pl.BlockSpec(memory_space=pltpu.SMEM),
             pl.BlockSpec((1, seq, dk), lambda b, h: (b0 + b, 0, h)),
             pl.BlockSpec((dk, seq), lambda b, h: (h, b0 + b)),
             pl.BlockSpec((1, seq, dv), lambda b, h: (b0 + b, 0, h)),
             pl.BlockSpec((1, seq, dv), lambda b, h: (b0 + b, 0, n_heads + h)),
             pl.BlockSpec((1, 1, dv), lambda b, h: (h, 0, 0))]
    args = [log_gamma, q3, kt, vg3, vg3, gn.reshape(n_heads, 1, dv)]
    st = pl.BlockSpec((1, 1, 1, dk, dv), lambda b, h: (b, 0, h, 0, 0))
    if has_init:
        specs += [st, st]
        args += [s_f0, s_b0]
    out_specs = [pl.BlockSpec((1, seq, dv), lambda b, h: (b, 0, h))]
    out_shape = [jax.ShapeDtypeStruct((nb, seq, n_heads * dv), BF16)]
    if emit_state:
        out_specs += [st, st]
        out_shape += [jax.ShapeDtypeStruct((nb, 1, n_heads, dk, dv), F32)] * 2
    gs = pl.GridSpec(grid=(nb, n_heads), in_specs=specs, out_specs=out_specs,
                     scratch_shapes=[pltpu.VMEM((seq, dv), F32), pltpu.VMEM((2, dk, dv), F32)])
    ch = 2 * RET_CHUNK if seq % (4 * RET_CHUNK) == 0 else RET_CHUNK
    outs = pl.pallas_call(
        functools.partial(_ret_kernel, ch=ch, nc=seq // ch, dk=dk, has_init=has_init, emit_state=emit_state),
        grid_spec=gs, out_shape=out_shape, name="retention",
        compiler_params=pltpu.CompilerParams(dimension_semantics=("arbitrary", "arbitrary"),
                                             vmem_limit_bytes=VMEM_LIMIT))(*args)
    y = outs[0].reshape(nb * seq, n_heads * dv)
    return (y,) + tuple(outs[1:])


def _ffn_up_kernel(idx_ref, h_hbm, w1_ref, w3_ref, o_ref, xf_scr, xb_scr, sem, *, tm, nf):
    tile = pl.program_id(0)
    f = pl.program_id(1)
    slot = tile % 2
    per = tm // nf

    def row_copy(row, dst_slot, r):
        return pltpu.make_async_copy(h_hbm.at[pl.ds(row, 1)], xf_scr.at[dst_slot, pl.ds(r, 1)], sem.at[dst_slot])

    def wait_rows(src_slot):
        pltpu.make_async_copy(h_hbm.at[pl.ds(0, tm)], xf_scr.at[src_slot], sem.at[src_slot]).wait()

    @pl.when(jnp.logical_and(tile == 0, f == 0))
    def _():
        def issue(r, carry):
            row_copy(idx_ref[r], 0, r).start()
            return carry

        lax.fori_loop(0, tm, issue, 0)

    @pl.when(f == 0)
    def _():
        wait_rows(slot)
        xb_scr[...] = xf_scr[slot].astype(BF16)

    base = (tile + 1) * tm + f * per
    for r in range(per):
        row_copy(idx_ref[base + r], 1 - slot, f * per + r).start()

    x = xb_scr[...]
    a = _bdot(x, w1_ref[0, 0].astype(BF16))
    b = _bdot(x, w3_ref[0, 0].astype(BF16))
    o_ref[...] = (_silu(a) * b).astype(o_ref.dtype)

    @pl.when(jnp.logical_and(tile == pl.num_programs(0) - 1, f == nf - 1))
    def _():
        wait_rows(1 - slot)


def ffn_up(h, rows, w1, w3, layer, tm, tiles_per_expert):
    _, n_exp, d, ff = w1.shape
    nslots = rows.shape[0]
    tf = _pick(ff, (512, 256, 128))
    nf = ff // tf
    tpe = tiles_per_expert
    rows = jnp.pad(rows, (0, tm))
    wspec = pl.BlockSpec((1, 1, d, tf), lambda t, f, idx: (layer, t // tpe, 0, f))
    return _call(functools.partial(_ffn_up_kernel, tm=tm, nf=nf), grid=(nslots // tm, nf), nsp=1,
                 in_specs=[pl.BlockSpec(memory_space=pl.ANY), wspec, wspec],
                 out_specs=pl.BlockSpec((tm, tf), lambda t, f, idx: (t, f)),
                 out_shape=jax.ShapeDtypeStruct((nslots, ff), BF16),
                 scratch=[pltpu.VMEM((2, tm, d), F32), pltpu.VMEM((tm, d), BF16), pltpu.SemaphoreType.DMA((2,))],
                 name="ffn_up")(rows, h, w1, w3)


def _ffn_down_kernel(x_ref, w_ref, g_ref, o_ref):
    acc = _bdot(x_ref[...], w_ref[0, 0].astype(BF16))
    g = g_ref[...]
    for c in range(acc.shape[1] // LANE):
        o_ref[:, c * LANE:(c + 1) * LANE] = acc[:, c * LANE:(c + 1) * LANE] * g


def ffn_down(hm, w2, layer, gates, tm, tiles_per_expert):
    _, n_exp, ff, d = w2.shape
    nslots = hm.shape[0]
    tn = _pick(d, (512, 256, 128))
    tpe = tiles_per_expert
    return _call(_ffn_down_kernel, grid=(nslots // tm, d // tn),
                 in_specs=[pl.BlockSpec((tm, ff), lambda t, j: (t, 0)),
                           pl.BlockSpec((1, 1, ff, tn), lambda t, j: (layer, t // tpe, 0, j)),
                           pl.BlockSpec((tm, LANE), lambda t, j: (t, 0))],
                 out_specs=pl.BlockSpec((tm, tn), lambda t, j: (t, j)),
                 out_shape=jax.ShapeDtypeStruct((nslots, d), F32), name="ffn_down")(hm, w2, gates)


SLOT_GROUP = 8
COMBINE_BATCH = 32


def _combine_kernel(flat_ref, pstart_ref, cidx_ref, y_hbm, t_hbm, x_ref, g_ref, *refs, nsplit):
    nout = 1 if nsplit is None else 2
    o_refs = refs[:nout]
    z_scr, tz_scr, acc_scr, st_ref, sem = refs[nout:]
    i = pl.program_id(0)
    tl = x_ref.shape[0]
    gb = COMBINE_BATCH
    rows_b = gb * SLOT_GROUP
    p0, p1 = pstart_ref[i], pstart_ref[i + 1]
    npos = p1 - p0
    nbatch = (npos + gb - 1) // gb
    next_has = jnp.logical_and(i + 1 < pl.num_programs(0), pstart_ref[i + 2] > p1)

    @pl.when(i == 0)
    def _():
        st_ref[0] = 0
        st_ref[1] = 0

    s0 = st_ref[0]
    started = st_ref[1]

    def gather(pbase, slot):
        def issue(k, carry):
            r = pl.multiple_of(flat_ref[pbase + k] * SLOT_GROUP, SLOT_GROUP)
            ko = pl.multiple_of(k * SLOT_GROUP, SLOT_GROUP)
            pltpu.make_async_copy(y_hbm.at[pl.ds(r, SLOT_GROUP)], z_scr.at[slot, pl.ds(ko, SLOT_GROUP)],
                                  sem.at[slot]).start()
            pltpu.make_async_copy(t_hbm.at[pl.ds(r, SLOT_GROUP)], tz_scr.at[slot, pl.ds(ko, SLOT_GROUP)],
                                  sem.at[slot]).start()
            return carry

        lax.fori_loop(0, gb, issue, 0)

    @pl.when(jnp.logical_and(nbatch > 0, started == 0))
    def _():
        gather(p0, s0)

    acc_scr[...] = jnp.zeros_like(acc_scr)
    t_col = (i * tl + lax.broadcasted_iota(I32, (tl, 1), 0)).astype(F32)
    lane_j = lax.broadcasted_iota(I32, (1, rows_b), 1)

    def batch(bi, carry):
        slot = (s0 + bi) % 2

        @pl.when(bi + 1 < nbatch)
        def _():
            gather(p0 + (bi + 1) * gb, 1 - slot)

        @pl.when(jnp.logical_and(bi + 1 == nbatch, next_has))
        def _():
            gather(p1, 1 - slot)

        pltpu.make_async_copy(y_hbm.at[pl.ds(0, rows_b)], z_scr.at[slot], sem.at[slot]).wait()
        pltpu.make_async_copy(t_hbm.at[pl.ds(0, rows_b)], tz_scr.at[slot], sem.at[slot]).wait()
        tok_row = tz_scr[slot].T[0:1, :]
        tok_row = jnp.where(lane_j < (npos - bi * gb) * SLOT_GROUP, tok_row, -1.0)
        onehot = jnp.where(tok_row == t_col, 1.0, 0.0).astype(BF16)
        zh, zl = _split(z_scr[slot])
        acc_scr[...] += _bdot(onehot, zh) + _bdot(onehot, zl)
        return carry

    lax.fori_loop(0, nbatch, batch, 0)

    @pl.when(nbatch > 0)
    def _():
        st_ref[0] = (s0 + nbatch) % 2
        st_ref[1] = next_has.astype(I32)

    @pl.when(nbatch == 0)
    def _():
        st_ref[1] = 0

    out = x_ref[...] + g_ref[0, 0] * acc_scr[...]
    if nsplit is None:
        o_refs[0][...] = out
    else:
        @pl.when(i < nsplit)
        def _():
            o_refs[0][...] = out

        @pl.when(i >= nsplit)
        def _():
            o_refs[1][...] = out


def moe_combine(x, y, toks, flat, pstart, mods, cidx, k_gate, split=None):
    t, d = x.shape
    tl = TOK_TILE
    blk = pl.BlockSpec((tl, d), lambda i, *_: (i, 0))
    if split is None:
        nsplit, out_specs, out_shape = None, blk, jax.ShapeDtypeStruct((t, d), F32)
    else:
        nsplit, n1 = split // tl, (t - split) // tl
        out_specs = [pl.BlockSpec((tl, d), lambda i, *_: (jnp.minimum(i, nsplit - 1), 0)),
                     pl.BlockSpec((tl, d), lambda i, *_: (jnp.clip(i - nsplit, 0, n1 - 1), 0))]
        out_shape = [jax.ShapeDtypeStruct((split, d), F32), jax.ShapeDtypeStruct((t - split, d), F32)]
    rows_b = COMBINE_BATCH * SLOT_GROUP
    return _call(functools.partial(_combine_kernel, nsplit=nsplit), grid=(t // tl,), nsp=3,
                 in_specs=[pl.BlockSpec(memory_space=pl.ANY), pl.BlockSpec(memory_space=pl.ANY), blk,
                           pl.BlockSpec((1, 1, 1, d), lambda i, f, p, c: (c[i], k_gate, 0, 0))],
                 out_specs=out_specs, out_shape=out_shape,
                 scratch=[pltpu.VMEM((2, rows_b, d), F32), pltpu.VMEM((2, rows_b, LANE), F32),
                          pltpu.VMEM((tl, d), F32), pltpu.SMEM((2,), I32), pltpu.SemaphoreType.DMA((2,))],
                 name="moe_combine")(flat, pstart, cidx, y, toks, x, mods)


def _route(aff_t, groups):
    n_exp = aff_t.shape[0]
    caps = [CAPACITY_FACTOR * seq // n_exp for _, _, seq in groups]
    slots = sum(nb * cap for (_, nb, _), cap in zip(groups, caps))
    e_ids = jnp.arange(n_exp, dtype=I32)[:, None, None]
    rows, gates, g0s, ngs = [], [], [], []
    off = 0
    for (row0, nb, seq), cap in zip(groups, caps):
        a = lax.slice_in_dim(aff_t, row0, row0 + nb * seq, axis=1).reshape(n_exp, nb, seq)
        gate, idx = lax.top_k(a, cap)
        idx, gate = lax.sort((idx.astype(I32), gate), dimension=2, num_keys=1)
        b_ids = jnp.arange(nb, dtype=I32)[None, :, None]
        rows.append((row0 + b_ids * seq + idx).reshape(n_exp, nb * cap))
        gates.append(gate.reshape(n_exp, nb * cap))
        ntile = seq // TOK_TILE
        bounds = jnp.arange(ntile + 1, dtype=I32) * TOK_TILE
        cnt = jnp.sum(idx[:, :, None, :] < bounds[None, None, :, None], axis=-1, dtype=I32)
        base = e_ids * slots + off + b_ids * cap
        lo, hi = base + cnt[..., :-1], base + cnt[..., 1:]
        g0 = lo // SLOT_GROUP
        ng = jnp.where(hi > lo, (hi + SLOT_GROUP - 1) // SLOT_GROUP - g0, 0)
        g0s.append(g0.transpose(1, 2, 0).reshape(nb * ntile, n_exp))
        ngs.append(ng.transpose(1, 2, 0).reshape(nb * ntile, n_exp))
        off += nb * cap
    g0 = jnp.concatenate(g0s, axis=0).reshape(-1)
    ng = jnp.concatenate(ngs, axis=0).reshape(-1)
    ntiles = g0.shape[0] // n_exp
    cum = jnp.cumsum(ng)
    start = cum - ng
    pstart = jnp.concatenate([start[::n_exp], cum[-1:], cum[-1:]]).astype(I32)
    pmax = n_exp * slots // SLOT_GROUP + 2 * ntiles * n_exp + COMBINE_BATCH
    p = jnp.arange(pmax, dtype=I32)
    delta = g0 - start
    step = delta - jnp.concatenate([jnp.zeros((1,), I32), delta[:-1]])
    offs = jnp.sum(jnp.where(p[:, None] >= start[None, :], step[None, :], 0), axis=1, dtype=I32)
    flat = jnp.where(p < cum[-1], p + offs, 0).astype(I32)
    return jnp.concatenate(rows, axis=1), jnp.concatenate(gates, axis=1), flat, pstart


def ec_moe(x, gain, mods, cidx, router, w1, w3, w2, layer, groups, split=None):
    n_exp = router.shape[1]
    h, aff_t = norm_mod(x, gain, mods, cidx, 3, 4, router_t=router.T)
    rows, gates, flat, pstart = _route(aff_t, groups)
    slots = rows.shape[1]
    tm = _pick(slots, (1024, 512, 256, 128, 64, 32, 16, 8))
    tpe = slots // tm
    rows_f = rows.reshape(-1)
    hm = ffn_up(h, rows_f, w1, w3, layer, tm, tpe)
    lanes = lambda v: jnp.broadcast_to(v.reshape(-1, 1), (n_exp * slots, LANE))
    y = ffn_down(hm, w2, layer, lanes(gates), tm, tpe)
    return moe_combine(x, y, lanes(rows_f.astype(F32)), flat, pstart, mods, cidx, 5, split)


def kernel(x_prompt, x_sample, cache_attn_k, cache_attn_v, state_ret_fwd, state_ret_bwd, c, c_ctx, ada_w, ada_b, norm1_g, norm2_g, ev_w_in, ev_w_out, hy_sconv_w, hy_sconv_b, hy_ffn_w1, hy_ffn_b1, hy_ffn_w2, hy_ffn_b2, hy_ffn_w3, hy_freq, hy_bias, at_q_norm, at_k_norm, at_sink, od_w_in, od_w_out, ret_decay, ret_gn, moe_router, moe_w1, moe_w3, moe_w2):
    bp, lp, d = x_prompt.shape
    bs, ls, _ = x_sample.shape
    depth = ada_w.shape[0]
    tp, ts = bp * lp, bs * ls
    t = tp + ts
    groups = ((0, bp, lp), (tp, bs, ls))
    a_width = hy_bias.shape[-1]
    hd = at_q_norm.shape[-1]
    n_heads = at_sink.shape[-1]
    n_kv = cache_attn_k.shape[3]
    c_heads, c_dv = ret_gn.shape[1], ret_gn.shape[2]
    c_dk = (od_w_in.shape[-1] - 2 * c_heads * c_dv) // (2 * c_heads)

    ntp, nts = tp // TOK_TILE, ts // TOK_TILE
    tps, tss = lp // TOK_TILE, ls // TOK_TILE
    tile = np.arange(ntp + nts)
    cidx = jnp.asarray(np.where(tile < ntp, 0, 1 + (tile - ntp) // tss), I32)
    in_seq = np.where(tile < ntp, tile % tps, (tile - ntp) % tss)
    seq_tiles = np.where(tile < ntp, tps, tss)
    first = jnp.asarray(in_seq == 0, I32)
    last = jnp.asarray(in_seq == seq_tiles - 1, I32)
    tm_mm = _pick(math.gcd(tp, ls), (1024, 512, 256))

    cond = jnp.concatenate([c_ctx[None, :], c], axis=0)
    cond = jnp.pad(cond, ((0, (-cond.shape[0]) % 8), (0, 0)))
    mods_all = adaln_all(cond, ada_w, ada_b)
    x = (x_prompt.reshape(tp, d), x_sample.reshape(ts, d))

    new_k, new_v, new_sf, new_sb = [], [], [], []
    for l in range(depth):
        mods = mods_all[l].reshape(-1, 6, 1, d)
        j = l // 2
        h = norm_mod(x, norm1_g[l], mods, cidx, 0, 1)
        if l % 2 == 0:
            u = matmul([h], ev_w_in[j], out_dtype=F32, tm=tm_mm)
            z, x0 = hy_pre(u, hy_sconv_w[j], hy_sconv_b[j], first, last, a_width)
            convs = []
            for row0, nb, seq in groups:
                taps = hyena_taps(seq, hy_ffn_w1[j], hy_ffn_b1[j], hy_ffn_w2[j], hy_ffn_b2[j],
                                  hy_ffn_w3[j], hy_freq[j], a_width)
                conv_fn = long_conv_direct if seq <= 512 else long_conv_two_stage
                convs.append(conv_fn(z, row0, nb, seq, taps))
            ya = hy_post(x0, z, hy_bias[j], convs[0], convs[1])
            cos_s, sin_s = _rope_tables(ls, hd)
            cos_t = jnp.concatenate([jnp.ones((tp, hd), F32), jnp.tile(cos_s, (bs, 1))], axis=0)
            sin_t = jnp.concatenate([jnp.zeros((tp, hd), F32), jnp.tile(sin_s, (bs, 1))], axis=0)
            q, k, v, k_normed = qkv_prep(u, 3 * a_width, n_heads, n_kv, hd, at_q_norm[j], at_k_norm[j], cos_t, sin_t)
            yb_p = attn_context(q, k, v, at_sink[j], 0, bp, lp, n_kv, hd)
            past = cache_attn_k.shape[2]
            ck = cache_attn_k[:, j].reshape(bs, past, n_kv * hd)
            cv = cache_attn_v[:, j].reshape(bs, past, n_kv * hd)
            yb_s = attn_latent(q, k, v, ck, cv, at_sink[j], tp, bs, ls, n_kv, hd)
            x = matmul([ya, (yb_p, yb_s)], ev_w_out[j], out_dtype=F32, tm=tm_mm, res=x, mods=mods, cidx=cidx, k_gate=2)
            kv0 = 3 * a_width + n_heads * hd
            new_k.append(k_normed[:tp].reshape(bp, 1, lp, n_kv, hd))
            new_v.append(u[:tp, kv0 + n_kv * hd:kv0 + 2 * n_kv * hd].reshape(bp, 1, lp, n_kv, hd))
        else:
            nqk = c_heads * c_dk
            w_in = od_w_in[j]
            q = matmul([h], w_in, out_dtype=BF16, tm=tm_mm, ncols=nqk)
            kt = matmul_t(h, w_in[:, nqk:2 * nqk].T, out_dtype=BF16, tm=tm_mm)
            vg = matmul([h], w_in, out_dtype=BF16, tm=tm_mm, col0=2 * nqk)
            log_gamma = -jnp.exp(ret_decay[j].astype(F32))
            y_p, sf, sb = retention(q, kt, vg, 0, bp, lp, c_heads, c_dk, c_dv, log_gamma, ret_gn[j], emit_state=True)
            (y_s,) = retention(q, kt, vg, tp, bs, ls, c_heads, c_dk, c_dv, log_gamma, ret_gn[j],
                               s_f0=state_ret_fwd[:, j:j + 1], s_b0=state_ret_bwd[:, j:j + 1])
            x = matmul([(y_p, y_s)], od_w_out[j], out_dtype=F32, tm=min(tm_mm, 512), res=x, mods=mods, cidx=cidx,
                       k_gate=2)
            new_sf.append(sf)
            new_sb.append(sb)
        x = ec_moe(x, norm2_g[l], mods, cidx, moe_router[l], moe_w1, moe_w3, moe_w2, l, groups,
                   split=tp if l == depth - 1 else None)

    cat = lambda parts: jnp.concatenate(parts, axis=1)
    return (x[0].reshape(bp, lp, d), x[1].reshape(bs, ls, d), cat(new_k), cat(new_v), cat(new_sf), cat(new_sb))
```

```python
import functools
import math

import ml_dtypes
import numpy as np
import jax
import jax.numpy as jnp
from jax import lax
from jax.experimental import pallas as pl
from jax.experimental.pallas import tpu as pltpu

F32 = jnp.float32
BF16 = jnp.bfloat16
I32 = jnp.int32
HIGHEST = lax.Precision.HIGHEST

EPS = 1e-6
NEG_INF = -1e30
GRID_W = 64
WINDOW = 128
ATT_BLOCK = 128
RET_CHUNK = 128
ROPE_BASE = 10000.0
A_BANDS = 16
A_SHORT_DECAY_PCT = 0.3
A_LONG_DECAY_PCT = 1.5
A_DECAY_TARGET = 1e-2
CAPACITY_FACTOR = 2

TOK_TILE = 256
LANE = 128
FFT_N2 = 128
VMEM_LIMIT = 56 * 1024 * 1024


def _pick(n, cands):
    for c in cands:
        if n % c == 0:
            return c
    raise ValueError(f"no tile for {n} in {cands}")


def _call(kernel, *, grid, in_specs, out_specs, out_shape, scratch=(), nsp=0, name=None):
    gs = pltpu.PrefetchScalarGridSpec(num_scalar_prefetch=nsp, grid=grid, in_specs=in_specs,
                                      out_specs=out_specs, scratch_shapes=list(scratch))
    cp = pltpu.CompilerParams(dimension_semantics=("arbitrary",) * len(grid), vmem_limit_bytes=VMEM_LIMIT)
    return pl.pallas_call(kernel, grid_spec=gs, out_shape=out_shape, compiler_params=cp, name=name)


def _silu(x):
    return x / (1.0 + jnp.exp(-x))


def _bdot(a, b):
    return jnp.dot(a, b, preferred_element_type=F32)


def _split(x):
    hi = x.astype(BF16)
    lo = (x - hi.astype(F32)).astype(BF16)
    return hi, lo


def _dot3(mh, ml, x):
    xh, xl = _split(x)
    return _bdot(mh, xh) + _bdot(mh, xl) + _bdot(ml, xh)


def _adaln_kernel(c_ref, w_ref, b_ref, o_ref):
    s = _silu(c_ref[...])
    o_ref[0] = _bdot(s.astype(BF16), w_ref[0].astype(BF16)) + b_ref[0]


def adaln_all(cond, ada_w, ada_b):
    depth, d, n6 = ada_w.shape
    rows = cond.shape[0]
    tn = _pick(n6, (1024, 512, 256, 128))
    return _call(
        _adaln_kernel, grid=(depth, n6 // tn),
        in_specs=[pl.BlockSpec((rows, d), lambda l, j: (0, 0)),
                  pl.BlockSpec((1, d, tn), lambda l, j: (l, 0, j)),
                  pl.BlockSpec((1, 1, tn), lambda l, j: (l, 0, j))],
        out_specs=pl.BlockSpec((1, rows, tn), lambda l, j: (l, 0, j)),
        out_shape=jax.ShapeDtypeStruct((depth, rows, n6), F32), name="adaln",
    )(cond, ada_w, ada_b.reshape(depth, 1, n6))


def _row_sources(a, tm):
    if not isinstance(a, tuple):
        return [a], [lambda i: i], None
    a0, a1 = a
    n0, n1 = a0.shape[0] // tm, a1.shape[0] // tm
    return [a0, a1], [lambda i: jnp.minimum(i, n0 - 1), lambda i: jnp.clip(i - n0, 0, n1 - 1)], n0


def _load_rows(refs, i, split):
    if split is None:
        return refs[0][...]
    return jnp.where(i < split, refs[0][...], refs[1][...])


def _norm_mod(x, g_ref, sc_ref, sh_ref):
    ms = jnp.mean(x * x, axis=-1, keepdims=True)
    y = x * lax.rsqrt(ms + EPS) * g_ref[...]
    return y * (1.0 + sc_ref[0, 0]) + sh_ref[0, 0]


def _norm_mod_kernel(cidx_ref, *refs, nsrc, split):
    g_ref, sc_ref, sh_ref, o_ref = refs[nsrc:]
    x = _load_rows(refs[:nsrc], pl.program_id(0), split)
    o_ref[...] = _norm_mod(x, g_ref, sc_ref, sh_ref).astype(o_ref.dtype)


def _norm_router_kernel(cidx_ref, *refs, nsrc, split):
    g_ref, sc_ref, sh_ref, rt_ref, h_ref, aff_ref = refs[nsrc:]
    x = _load_rows(refs[:nsrc], pl.program_id(0), split)
    h = _norm_mod(x, g_ref, sc_ref, sh_ref)
    h_ref[...] = h
    nt = lambda a, b: lax.dot_general(a, b, (((1,), (1,)), ((), ())), preferred_element_type=F32)
    rh, rl = _split(rt_ref[...])
    hh, hl = _split(h)
    logits = nt(rh, hh) + nt(rh, hl) + nt(rl, hh)
    m = jnp.max(logits, axis=0, keepdims=True)
    e = jnp.exp(logits - m)
    aff_ref[...] = e / jnp.sum(e, axis=0, keepdims=True)


def norm_mod(x, gain, mods, cidx, k_shift, k_scale, *, router_t=None):
    tm = TOK_TILE
    srcs, fns, split = _row_sources(x, tm)
    t = sum(a.shape[0] for a in srcs)
    d = srcs[0].shape[1]
    specs = [pl.BlockSpec((tm, d), lambda i, c, f=f: (f(i), 0)) for f in fns]
    specs += [pl.BlockSpec((1, d), lambda i, c: (0, 0)),
              pl.BlockSpec((1, 1, 1, d), lambda i, c: (c[i], k_scale, 0, 0)),
              pl.BlockSpec((1, 1, 1, d), lambda i, c: (c[i], k_shift, 0, 0))]
    kw = dict(nsrc=len(srcs), split=split)
    if router_t is None:
        return _call(functools.partial(_norm_mod_kernel, **kw), grid=(t // tm,), nsp=1, in_specs=specs,
                     out_specs=pl.BlockSpec((tm, d), lambda i, c: (i, 0)),
                     out_shape=jax.ShapeDtypeStruct((t, d), BF16), name="norm_mod",
                     )(cidx, *srcs, gain.reshape(1, d), mods, mods)
    e = router_t.shape[0]
    specs.append(pl.BlockSpec((e, d), lambda i, c: (0, 0)))
    return _call(functools.partial(_norm_router_kernel, **kw), grid=(t // tm,), nsp=1, in_specs=specs,
                 out_specs=[pl.BlockSpec((tm, d), lambda i, c: (i, 0)),
                            pl.BlockSpec((e, tm), lambda i, c: (0, i))],
                 out_shape=[jax.ShapeDtypeStruct((t, d), F32), jax.ShapeDtypeStruct((e, t), F32)],
                 name="norm_router",
                 )(cidx, *srcs, gain.reshape(1, d), mods, mods, router_t)


def _mm_kernel(*refs, xcounts, xsplits, gated, rcount, rsplit):
    pos = 1 if gated else 0
    i = pl.program_id(1)
    x_refs = []
    for cnt in xcounts:
        x_refs.append(refs[pos:pos + cnt])
        pos += cnt
    nx = len(xcounts)
    w_refs = refs[pos:pos + nx]
    pos += nx
    if gated:
        res_refs, gate_ref = refs[pos:pos + rcount], refs[pos + rcount]
        pos += rcount + 1
    o_ref = refs[pos]
    wb_refs = refs[pos + 1:]

    @pl.when(i == 0)
    def _():
        for w_ref, wb in zip(w_refs, wb_refs):
            wb[...] = w_ref[...].astype(BF16)

    acc = None
    for xr, split, wb in zip(x_refs, xsplits, wb_refs):
        part = _bdot(_load_rows(xr, i, split), wb[...])
        acc = part if acc is None else acc + part
    if gated:
        acc = _load_rows(res_refs, i, rsplit) + gate_ref[0, 0] * acc
    o_ref[...] = acc.astype(o_ref.dtype)


def matmul(xs, w, *, out_dtype, tm, tn=512, col0=0, ncols=None, res=None, mods=None, cidx=None, k_gate=None):
    srcs = [_row_sources(x, tm) for x in xs]
    m = sum(a.shape[0] for a in srcs[0][0])
    kk = srcs[0][0][0].shape[1]
    nx = len(xs)
    assert w.shape[0] == nx * kk
    n = w.shape[1] - col0 if ncols is None else ncols
    tn = _pick(math.gcd(n, col0) if col0 else n, (tn, 512, 256, 128))
    cb0 = col0 // tn
    gated = res is not None
    if gated:
        imap = lambda f: (lambda j, i, c: f(j, i, c))
    else:
        imap = lambda f: (lambda j, i: f(j, i, None))
    specs, args = [], []
    for arrs, fns, _ in srcs:
        specs += [pl.BlockSpec((tm, kk), imap(lambda j, i, c, f=f: (f(i), 0))) for f in fns]
        args += arrs
    specs += [pl.BlockSpec((kk, tn), imap(lambda j, i, c, q=q: (q, cb0 + j))) for q in range(nx)]
    args += [w] * nx
    rcount, rsplit = 0, None
    if gated:
        assert mods.shape[-1] == n and col0 == 0
        rarrs, rfns, rsplit = _row_sources(res, tm)
        rcount = len(rarrs)
        specs += [pl.BlockSpec((tm, tn), imap(lambda j, i, c, f=f: (f(i), j))) for f in rfns]
        stride = tm // TOK_TILE
        specs.append(pl.BlockSpec((1, 1, 1, tn), imap(lambda j, i, c: (c[i * stride], k_gate, 0, j))))
        args = [cidx] + args + rarrs + [mods]
    kern = functools.partial(_mm_kernel, xcounts=tuple(len(s[0]) for s in srcs),
                             xsplits=tuple(s[2] for s in srcs), gated=gated, rcount=rcount, rsplit=rsplit)
    return _call(kern, grid=(n // tn, m // tm), nsp=1 if gated else 0, in_specs=specs,
                 out_specs=pl.BlockSpec((tm, tn), imap(lambda j, i, c: (i, j))),
                 out_shape=jax.ShapeDtypeStruct((m, n), out_dtype),
                 scratch=[pltpu.VMEM((kk, tn), BF16) for _ in xs], name="matmul",
                 )(*args)


def _mm_t_kernel(x_ref, w_ref, o_ref, wb):
    @pl.when(pl.program_id(1) == 0)
    def _():
        wb[...] = w_ref[...].astype(BF16)

    o_ref[...] = lax.dot_general(wb[...], x_ref[...], (((1,), (1,)), ((), ())),
                                 preferred_element_type=F32).astype(o_ref.dtype)


def matmul_t(x, wt, *, out_dtype, tm, tn=512):
    m, kk = x.shape
    n = wt.shape[0]
    tn = _pick(n, (tn, 512, 256, 128))
    return _call(_mm_t_kernel, grid=(n // tn, m // tm),
                 in_specs=[pl.BlockSpec((tm, kk), lambda j, i: (i, 0)), pl.BlockSpec((tn, kk), lambda j, i: (j, 0))],
                 out_specs=pl.BlockSpec((tn, tm), lambda j, i: (j, i)),
                 out_shape=jax.ShapeDtypeStruct((n, m), out_dtype),
                 scratch=[pltpu.VMEM((tn, kk), BF16)], name="matmul_t")(x, wt)


def _hy_pre_kernel(first_ref, last_ref, *refs):
    ins = refs[:15]
    z_ref, x0_ref = refs[15], refs[16]
    i = pl.program_id(0)
    is_first = first_ref[i] == 1
    is_last = last_ref[i] == 1

    def sconv(m_ref, p_ref, n_ref, w_ref, b_ref):
        x = m_ref[...]
        tl = x.shape[0]
        rows = lax.broadcasted_iota(I32, x.shape, 0)
        prev_row = jnp.where(is_first, 0.0, p_ref[7:8, :])
        next_row = jnp.where(is_last, 0.0, n_ref[0:1, :])
        xm1 = jnp.where(rows == 0, prev_row, pltpu.roll(x, 1, 0))
        xp1 = jnp.where(rows == tl - 1, next_row, pltpu.roll(x, tl - 1, 0))
        w = w_ref[...]
        return xm1 * w[0:1] + x * w[1:2] + xp1 * w[2:3] + b_ref[...]

    hv = sconv(ins[0], ins[1], ins[2], ins[9], ins[12])
    x0 = sconv(ins[3], ins[4], ins[5], ins[10], ins[13])
    x1 = sconv(ins[6], ins[7], ins[8], ins[11], ins[14])
    z_ref[...] = x1 * hv
    x0_ref[...] = x0


def hy_pre(u, sc_w, sc_b, first, last, a_width):
    t = u.shape[0]
    tl = TOK_TILE
    cb = _pick(a_width, (512, 256, 128))
    ncb = a_width // cb
    nt8 = t // 8
    specs = []
    for part in range(3):
        off = part * ncb
        specs += [pl.BlockSpec((tl, cb), lambda i, j, f, l, off=off: (i, off + j)),
                  pl.BlockSpec((8, cb), lambda i, j, f, l, off=off: (jnp.maximum(i * (tl // 8) - 1, 0), off + j)),
                  pl.BlockSpec((8, cb), lambda i, j, f, l, off=off: (jnp.minimum((i + 1) * (tl // 8), nt8 - 1), off + j))]
    for part in range(3):
        off = part * ncb
        specs.append(pl.BlockSpec((3, cb), lambda i, j, f, l, off=off: (0, off + j)))
    for part in range(3):
        off = part * ncb
        specs.append(pl.BlockSpec((1, cb), lambda i, j, f, l, off=off: (0, off + j)))
    out_spec = pl.BlockSpec((tl, cb), lambda i, j, f, l: (i, j))
    scb = sc_b.reshape(1, -1)
    return _call(_hy_pre_kernel, grid=(t // tl, ncb), nsp=2, in_specs=specs,
                 out_specs=[out_spec, out_spec],
                 out_shape=[jax.ShapeDtypeStruct((t, a_width), F32)] * 2, name="hy_pre",
                 )(first, last, *([u] * 9), sc_w, sc_w, sc_w, scb, scb, scb)


def _filter_kernel(fa_ref, fb_ref, w1_ref, b1_ref, w2_ref, b2_ref, fr_ref, w3f_ref, w3b_ref, dl_ref, o_ref):
    hdot = functools.partial(jnp.dot, precision=HIGHEST, preferred_element_type=F32)

    def taps(feats, w3_ref):
        h = jnp.sin(fr_ref[0:1, :] * (hdot(feats, w1_ref[...]) + b1_ref[...]))
        h = jnp.sin(fr_ref[1:2, :] * (hdot(h, w2_ref[...]) + b2_ref[...]))
        return hdot(h, w3_ref[...]) * jnp.exp(-feats[:, 0:1] * dl_ref[...])

    hf = taps(fa_ref[...], w3f_ref)
    hb = taps(fb_ref[...], w3b_ref)
    s = (jnp.sum(jnp.abs(hf), axis=0, keepdims=True) + jnp.sum(jnp.abs(hb), axis=0, keepdims=True)) + EPS
    o_ref[0] = hf / s
    o_ref[1] = jnp.where(lax.broadcasted_iota(I32, hb.shape, 0) == 0, 0.0, hb / s)


def hyena_taps(seq, w1, b1, w2, b2, w3, freq, a_width):
    t = jnp.arange(seq, dtype=F32) / seq
    bands = jnp.linspace(1e-4, A_BANDS - 1, A_BANDS, dtype=F32)
    ang = 2.0 * math.pi * t[:, None] * bands[None, :]
    feats = jnp.concatenate([t[:, None], jnp.cos(ang), -jnp.sin(ang)], axis=-1)
    emb = feats.shape[1]
    feats = jnp.pad(feats, ((0, 0), (0, LANE - emb)))
    feats_b = jnp.roll(feats[::-1], 1, axis=0)
    w1p = jnp.pad(w1, ((0, LANE - emb), (0, 0)))
    max_decay = math.log(A_DECAY_TARGET) / A_SHORT_DECAY_PCT
    min_decay = math.log(A_DECAY_TARGET) / A_LONG_DECAY_PCT
    deltas = jnp.abs(jnp.linspace(min_decay, max_decay, a_width, dtype=F32)).reshape(1, a_width)
    ffn = w2.shape[0]
    cb = _pick(a_width, (256, 128))
    ncb = a_width // cb
    full = lambda shape: pl.BlockSpec(shape, lambda j: (0,) * len(shape))
    taps = _call(_filter_kernel, grid=(ncb,),
                 in_specs=[full((seq, LANE)), full((seq, LANE)), full((LANE, ffn)), full((1, ffn)), full((ffn, ffn)),
                           full((1, ffn)), full((2, ffn)),
                           pl.BlockSpec((ffn, cb), lambda j: (0, j)),
                           pl.BlockSpec((ffn, cb), lambda j: (0, ncb + j)),
                           pl.BlockSpec((1, cb), lambda j: (0, j))],
                 out_specs=pl.BlockSpec((2, seq, cb), lambda j: (0, 0, j)),
                 out_shape=jax.ShapeDtypeStruct((2, seq, a_width), F32), name="hy_filter",
                 )(feats, feats_b, w1p, b1.reshape(1, ffn), w2, b2.reshape(1, ffn), freq, w3, w3, deltas)
    return taps.reshape(2 * seq, a_width)


def _hilo(x):
    x32 = np.asarray(x, np.float32)
    hi = x32.astype(ml_dtypes.bfloat16)
    lo = (x32 - hi.astype(np.float32)).astype(ml_dtypes.bfloat16)
    return hi, lo


@functools.lru_cache(maxsize=None)
def _dft_direct_consts(seq):
    n = 2 * seq
    k = np.arange(n)
    ang = 2.0 * np.pi * ((k[:, None] * k[None, :]) % n) / n
    c, s = np.cos(ang), np.sin(ang)
    fwd = np.block([[c[:, :seq], s[:, :seq]], [-s[:, :seq], c[:, :seq]]])
    taps = np.concatenate([c, -s], axis=0)
    inv = np.block([[c[:seq, :], -s[:seq, :]], [s[:seq, :], c[:seq, :]]]) / n
    return _hilo(fwd), _hilo(taps), _hilo(inv)


@functools.lru_cache(maxsize=None)
def _dft_two_stage_consts(seq):
    n = 2 * seq
    n2c = FFT_N2
    n1c = n // n2c
    h = n1c // 2
    n2 = np.arange(n2c)[:, None, None]
    k1 = np.arange(n1c)[None, :, None]
    n1 = np.arange(n1c)[None, None, :]
    psi = 2.0 * np.pi * ((n1 * k1 * n2c + n2 * k1) % n) / n
    c, s = np.cos(psi), np.sin(psi)
    m1d = np.concatenate([np.concatenate([c[:, :, :h], s[:, :, :h]], axis=2),
                          np.concatenate([-s[:, :, :h], c[:, :, :h]], axis=2)], axis=1)
    m1t = np.concatenate([c, -s], axis=1)
    ct = np.swapaxes(c, 1, 2)[:, :h, :]
    st = np.swapaxes(s, 1, 2)[:, :h, :]
    m1i = np.concatenate([np.concatenate([ct, -st], axis=2),
                          np.concatenate([st, ct], axis=2)], axis=1) / n
    kk = np.arange(n2c)
    th = 2.0 * np.pi * ((kk[:, None] * kk[None, :]) % n2c) / n2c
    c2, s2 = np.cos(th), np.sin(th)
    m2 = np.block([[c2, s2], [-s2, c2]])
    m2i = np.block([[c2, -s2], [s2, c2]])
    return _hilo(m1d), _hilo(m1t), _hilo(m1i), _hilo(m2), _hilo(m2i)


def _cmul_stacked(x, hs, half):
    xr, xi = x[:half], x[half:]
    hr, hi = hs[:half], hs[half:]
    return jnp.concatenate([xr * hr - xi * hi, xr * hi + xi * hr], axis=0)


def _const_mm3_kernel(mh_ref, ml_ref, x_ref, o_ref):
    o_ref[...] = _dot3(mh_ref[...], ml_ref[...], x_ref[...])


def const_mm3(mh, ml, x):
    m, k = mh.shape
    c = x.shape[1]
    cb = _pick(c, (256, 128))
    return _call(_const_mm3_kernel, grid=(c // cb,),
                 in_specs=[pl.BlockSpec((m, k), lambda j: (0, 0)), pl.BlockSpec((m, k), lambda j: (0, 0)),
                           pl.BlockSpec((k, cb), lambda j: (0, j))],
                 out_specs=pl.BlockSpec((m, cb), lambda j: (0, j)),
                 out_shape=jax.ShapeDtypeStruct((m, c), F32), name="dft_taps")(mh, ml, x)


def _conv_direct_kernel(fh_ref, fl_ref, gh_ref, gl_ref, zr_ref, zi_ref, h_ref, yr_ref, yi_ref, *, seq):
    x = jnp.concatenate([zr_ref[...], zi_ref[...]], axis=0)
    spec = _dot3(fh_ref[...], fl_ref[...], x)
    y = _dot3(gh_ref[...], gl_ref[...], _cmul_stacked(spec, h_ref[...], 2 * seq))
    yr_ref[...] = y[:seq]
    yi_ref[...] = y[seq:]


def long_conv_direct(z, row0, nb, seq, taps):
    a_width = z.shape[1]
    (fh, fl), (th, tl_), (gh, gl) = _dft_direct_consts(seq)
    spec_h = const_mm3(jnp.asarray(th), jnp.asarray(tl_), taps)
    cb = _pick(a_width, (256, 128))
    half = nb // 2
    b0 = row0 // seq
    full = lambda a: pl.BlockSpec(a.shape, lambda j, p: (0, 0))
    yr, yi = _call(
        functools.partial(_conv_direct_kernel, seq=seq), grid=(a_width // cb, half),
        in_specs=[full(fh), full(fl), full(gh), full(gl),
                  pl.BlockSpec((seq, cb), lambda j, p: (b0 + p, j)),
                  pl.BlockSpec((seq, cb), lambda j, p: (b0 + half + p, j)),
                  pl.BlockSpec((4 * seq, cb), lambda j, p: (0, j))],
        out_specs=[pl.BlockSpec((seq, cb), lambda j, p: (p, j))] * 2,
        out_shape=[jax.ShapeDtypeStruct((half * seq, a_width), F32)] * 2, name="conv_direct",
    )(jnp.asarray(fh), jnp.asarray(fl), jnp.asarray(gh), jnp.asarray(gl), z, z, spec_h)
    return jnp.concatenate([yr, yi], axis=0)


def _fft_s1_kernel(mh_ref, ml_ref, *refs, g, packed):
    o_ref = refs[-1]
    for j in range(g):
        if packed:
            x = jnp.concatenate([refs[0][0, j], refs[1][0, j]], axis=0)
        else:
            x = refs[0][0, j]
        o_ref[0, j] = _dot3(mh_ref[j], ml_ref[j], x)


def _fft_s1(mh, ml, zp, *, packed):
    n2c, m_rows, n1c = mh.shape
    bsz, _, rows, c = zp.shape
    p = bsz // 2 if packed else bsz
    cb = _pick(c, (256, 128))
    g = 16
    cspec = pl.BlockSpec((g, m_rows, n1c), lambda q, j, t: (t, 0, 0))
    zspecs = [pl.BlockSpec((1, g, rows, cb), lambda q, j, t: (q, t, 0, j))]
    args = [zp]
    if packed:
        zspecs.append(pl.BlockSpec((1, g, rows, cb), lambda q, j, t: (q + p, t, 0, j)))
        args.append(zp)
    return _call(functools.partial(_fft_s1_kernel, g=g, packed=packed), grid=(p, c // cb, n2c // g),
                 in_specs=[cspec, cspec] + zspecs,
                 out_specs=pl.BlockSpec((1, g, m_rows, cb), lambda q, j, t: (q, t, 0, j)),
                 out_shape=jax.ShapeDtypeStruct((p, n2c, m_rows, c), F32), name="fft_s1",
                 )(jnp.asarray(mh), jnp.asarray(ml), *args)


def _fft_s2_kernel(mh_ref, ml_ref, ih_ref, il_ref, *refs, g, conv):
    a_ref, o_ref = refs[0], refs[-1]
    half = a_ref.shape[2] // 2
    for j in range(g):
        spec = _dot3(mh_ref[...], ml_ref[...], a_ref[0, j])
        if conv:
            spec = _dot3(ih_ref[...], il_ref[...], _cmul_stacked(spec, refs[1][0, j], half))
        o_ref[0, j] = spec


def _fft_s2(m2, m2i, at, spec_h=None):
    p, n1c, rows, c = at.shape
    cb = _pick(c, (256, 128))
    g = _pick(n1c, (8, 4, 2, 1))
    conv = spec_h is not None
    mspec = pl.BlockSpec((rows, rows), lambda q, j, t: (0, 0))
    blk = pl.BlockSpec((1, g, rows, cb), lambda q, j, t: (q, t, 0, j))
    specs = [mspec] * 4 + [blk]
    args = [jnp.asarray(m2[0]), jnp.asarray(m2[1]), jnp.asarray(m2i[0]), jnp.asarray(m2i[1]), at]
    if conv:
        specs.append(pl.BlockSpec((1, g, rows, cb), lambda q, j, t: (0, t, 0, j)))
        args.append(spec_h)
    return _call(functools.partial(_fft_s2_kernel, g=g, conv=conv), grid=(p, c // cb, n1c // g),
                 in_specs=specs, out_specs=blk, out_shape=jax.ShapeDtypeStruct(at.shape, F32), name="fft_s2",
                 )(*args)


def _fft_s3_kernel(mh_ref, ml_ref, b_ref, yr_ref, yi_ref, *, g):
    half = yr_ref.shape[2]
    for j in range(g):
        y = _dot3(mh_ref[j], ml_ref[j], b_ref[0, j])
        yr_ref[0, j] = y[:half]
        yi_ref[0, j] = y[half:]


def _fft_s3(mh, ml, bt):
    n2c, n1c, rows = mh.shape
    p, _, _, c = bt.shape
    cb = _pick(c, (256, 128))
    g = 16
    h = n1c // 2
    oblk = pl.BlockSpec((1, g, h, cb), lambda q, j, t: (q, t, 0, j))
    cspec = pl.BlockSpec((g, n1c, rows), lambda q, j, t: (t, 0, 0))
    return _call(functools.partial(_fft_s3_kernel, g=g), grid=(p, c // cb, n2c // g),
                 in_specs=[cspec, cspec, pl.BlockSpec((1, g, rows, cb), lambda q, j, t: (q, t, 0, j))],
                 out_specs=[oblk, oblk],
                 out_shape=[jax.ShapeDtypeStruct((p, n2c, h, c), F32)] * 2, name="fft_s3",
                 )(jnp.asarray(mh), jnp.asarray(ml), bt)


def _swap_digits(a):
    p, x, y2, c = a.shape
    y = y2 // 2
    return a.reshape(p, x, 2, y, c).transpose(0, 3, 2, 1, 4).reshape(p, y, 2 * x, c)


def long_conv_two_stage(z, row0, nb, seq, taps):
    a_width = z.shape[1]
    m1d, m1t, m1i, m2, m2i = _dft_two_stage_consts(seq)
    n2c = FFT_N2
    n1c = 2 * seq // n2c
    taps_p = taps.reshape(1, n1c, n2c, a_width).transpose(0, 2, 1, 3)
    spec_h = _fft_s2(m2, m2i, _swap_digits(_fft_s1(m1t[0], m1t[1], taps_p, packed=False)))
    zs = lax.slice_in_dim(z, row0, row0 + nb * seq, axis=0)
    zp = zs.reshape(nb, n1c // 2, n2c, a_width).transpose(0, 2, 1, 3)
    a = _swap_digits(_fft_s1(m1d[0], m1d[1], zp, packed=True))
    b = _swap_digits(_fft_s2(m2, m2i, a, spec_h))
    yr, yi = _fft_s3(m1i[0], m1i[1], b)
    y = jnp.concatenate([yr, yi], axis=0)
    return y.transpose(0, 2, 1, 3).reshape(nb * seq, a_width)


def _hy_post_kernel(x0_ref, z_ref, bias_ref, cp_ref, cs_ref, o_ref, *, ntp):
    conv = jnp.where(pl.program_id(0) < ntp, cp_ref[...], cs_ref[...])
    o_ref[...] = (x0_ref[...] * (conv + z_ref[...] * bias_ref[...])).astype(o_ref.dtype)


def hy_post(x0, z, bias, conv_p, conv_s):
    t, a_width = x0.shape
    tl = TOK_TILE
    ntp = conv_p.shape[0] // tl
    nts = conv_s.shape[0] // tl
    cb = _pick(a_width, (512, 256, 128))
    blk = pl.BlockSpec((tl, cb), lambda i, j: (i, j))
    return _call(functools.partial(_hy_post_kernel, ntp=ntp), grid=(t // tl, a_width // cb),
                 in_specs=[blk, blk, pl.BlockSpec((1, cb), lambda i, j: (0, j)),
                           pl.BlockSpec((tl, cb), lambda i, j: (jnp.minimum(i, ntp - 1), j)),
                           pl.BlockSpec((tl, cb), lambda i, j: (jnp.clip(i - ntp, 0, nts - 1), j))],
                 out_specs=blk, out_shape=jax.ShapeDtypeStruct((t, a_width), BF16), name="hy_post",
                 )(x0, z, bias.reshape(1, a_width), conv_p, conv_s)


def _qkv_prep_kernel(q_ref, k_ref, v_ref, qn_ref, kn_ref, cos_ref, sin_ref, qo_ref, ko_ref, vo_ref, kc_ref, *, hd, scale):
    cos = cos_ref[...]
    sin = sin_ref[...]
    lane = lax.broadcasted_iota(I32, cos.shape, 1)
    low = (lane % (hd // 2)) < (hd // 4)

    def norm(x, g):
        return x * lax.rsqrt(jnp.mean(x * x, axis=-1, keepdims=True) + EPS) * g

    def rope(x):
        partner = jnp.where(low, pltpu.roll(x, hd - hd // 4, 1), pltpu.roll(x, hd // 4, 1))
        return x * cos + partner * sin

    for h in range(q_ref.shape[1] // hd):
        sl = slice(h * hd, (h + 1) * hd)
        qo_ref[:, sl] = (rope(norm(q_ref[:, sl], qn_ref[...])) * scale).astype(qo_ref.dtype)
    for h in range(k_ref.shape[1] // hd):
        sl = slice(h * hd, (h + 1) * hd)
        kn = norm(k_ref[:, sl], kn_ref[...])
        kc_ref[:, sl] = kn
        ko_ref[:, sl] = rope(kn).astype(ko_ref.dtype)
    vo_ref[...] = v_ref[...].astype(vo_ref.dtype)


def qkv_prep(u, col0, n_heads, n_kv, hd, qn, kn, cos_t, sin_t):
    t = u.shape[0]
    tl = TOK_TILE
    qw, kw = n_heads * hd, n_kv * hd
    assert col0 % qw == 0 and (col0 + qw) % kw == 0
    qb, kb = col0 // qw, (col0 + qw) // kw
    row = lambda w: pl.BlockSpec((tl, w), lambda i: (i, 0))
    vec = pl.BlockSpec((1, hd), lambda i: (0, 0))
    return _call(functools.partial(_qkv_prep_kernel, hd=hd, scale=hd ** -0.5), grid=(t // tl,),
                 in_specs=[pl.BlockSpec((tl, qw), lambda i: (i, qb)),
                           pl.BlockSpec((tl, kw), lambda i: (i, kb)),
                           pl.BlockSpec((tl, kw), lambda i: (i, kb + 1)),
                           vec, vec, row(hd), row(hd)],
                 out_specs=[row(qw), row(kw), row(kw), row(kw)],
                 out_shape=[jax.ShapeDtypeStruct((t, qw), BF16), jax.ShapeDtypeStruct((t, kw), BF16),
                            jax.ShapeDtypeStruct((t, kw), BF16), jax.ShapeDtypeStruct((t, kw), F32)],
                 name="qkv_prep")(u, u, u, qn.reshape(1, hd), kn.reshape(1, hd), cos_t, sin_t)


def _rope_tables(seq, hd):
    rows = seq // GRID_W
    row = jnp.repeat(jnp.arange(rows), GRID_W).astype(F32)
    col = jnp.tile(jnp.arange(GRID_W), rows).astype(F32)
    nf = hd // 4
    inv = jnp.exp(-math.log(ROPE_BASE) * jnp.arange(nf, dtype=F32) / nf)
    ang_r = row[:, None] * inv[None, :]
    ang_c = col[:, None] * inv[None, :]
    cos = jnp.concatenate([jnp.cos(ang_r)] * 2 + [jnp.cos(ang_c)] * 2, axis=-1)
    sin = jnp.concatenate([-jnp.sin(ang_r), jnp.sin(ang_r), -jnp.sin(ang_c), jnp.sin(ang_c)], axis=-1)
    return cos, sin


def _attn_kernel(sink_ref, q_ref, *refs, kinds, group, hd, nblk):
    nk = len(kinds)
    k_refs, v_refs, o_ref = refs[:nk], refs[nk:2 * nk], refs[2 * nk]
    kvg = pl.program_id(1)
    i = pl.program_id(2)
    tq = q_ref.shape[0]
    q = jnp.concatenate([q_ref[:, g * hd:(g + 1) * hd] for g in range(group)], axis=0)
    rows = lax.broadcasted_iota(I32, (group * tq, 1), 0)
    sink = jnp.full((group * tq, 1), sink_ref[kvg * group], F32)
    for g in range(1, group):
        sink = jnp.where(rows >= g * tq, sink_ref[kvg * group + g], sink)

    def load(ref):
        x = ref[0] if len(ref.shape) == 3 else ref[...]
        return x.astype(BF16)

    scores = []
    m = sink
    for kind, k_ref in zip(kinds, k_refs):
        s = lax.dot_general(q, load(k_ref), (((1,), (1,)), ((), ())), preferred_element_type=F32)
        if kind in ("prev", "next"):
            r = lax.broadcasted_iota(I32, s.shape, 0) % tq
            c = lax.broadcasted_iota(I32, s.shape, 1)
            if kind == "prev":
                ok = c >= r + jnp.where(i > 0, 0, 2 * tq)
            else:
                ok = c <= r - jnp.where(i < nblk - 1, 0, 2 * tq)
            s = jnp.where(ok, s, NEG_INF)
        scores.append(s)
        m = jnp.maximum(m, jnp.max(s, axis=-1, keepdims=True))
    den = jnp.exp(sink - m)
    acc = None
    for s, v_ref in zip(scores, v_refs):
        p = jnp.exp(s - m)
        den = den + jnp.sum(p, axis=-1, keepdims=True)
        pv = _bdot(p.astype(BF16), load(v_ref))
        acc = pv if acc is None else acc + pv
    out = acc / den
    o_ref[...] = jnp.concatenate([out[g * tq:(g + 1) * tq] for g in range(group)], axis=1).astype(o_ref.dtype)


def attn_context(q, k, v, sink, row0, nb, seq, n_kv, hd):
    group = q.shape[1] // (n_kv * hd)
    b0 = row0 // seq
    kv = pl.BlockSpec((seq, hd), lambda b, g, i, s: (b0 + b, g))
    return _call(functools.partial(_attn_kernel, kinds=("all",), group=group, hd=hd, nblk=1),
                 grid=(nb, n_kv, 1), nsp=1,
                 in_specs=[pl.BlockSpec((seq, group * hd), lambda b, g, i, s: (b0 + b, g)), kv, kv],
                 out_specs=pl.BlockSpec((seq, group * hd), lambda b, g, i, s: (b, g)),
                 out_shape=jax.ShapeDtypeStruct((nb * seq, q.shape[1]), BF16), name="attn_context",
                 )(sink, q, k, v)


def attn_latent(q, k, v, ck, cv, sink, row0, nb, seq, n_kv, hd):
    group = q.shape[1] // (n_kv * hd)
    blk = ATT_BLOCK
    nblk = seq // blk
    r0 = row0 // blk
    past = ck.shape[1]
    cur = lambda b, g, i, s: (r0 + b * nblk + i, g)
    prev = lambda b, g, i, s: (r0 + b * nblk + jnp.maximum(i - 1, 0), g)
    nxt = lambda b, g, i, s: (r0 + b * nblk + jnp.minimum(i + 1, nblk - 1), g)
    ctx = pl.BlockSpec((1, past, hd), lambda b, g, i, s: (b, 0, g))
    kvs = [pl.BlockSpec((blk, hd), f) for f in (prev, cur, nxt)]
    return _call(functools.partial(_attn_kernel, kinds=("prev", "cur", "next", "ctx"), group=group, hd=hd, nblk=nblk),
                 grid=(nb, n_kv, nblk), nsp=1,
                 in_specs=[pl.BlockSpec((blk, group * hd), cur)] + kvs + [ctx] + kvs + [ctx],
                 out_specs=pl.BlockSpec((blk, group * hd), lambda b, g, i, s: (b * nblk + i, g)),
                 out_shape=jax.ShapeDtypeStruct((nb * seq, q.shape[1]), BF16), name="attn_latent",
                 )(sink, q, k, k, k, ck, v, v, v, cv)


def _ret_kernel(lg_ref, q_ref, kt_ref, v_ref, g_ref, gn_ref, *refs, ch, nc, dk, has_init, emit_state):
    pos = 0
    if has_init:
        s0_refs = refs[0:2]
        pos = 2
    y_ref = refs[pos]
    pos += 1
    if emit_state:
        so_refs = refs[pos:pos + 2]
        pos += 2
    o_scr, s_scr = refs[pos], refs[pos + 1]
    h = pl.program_id(1)
    kscale = dk ** -0.5
    ii = lax.broadcasted_iota(I32, (ch, ch), 0).astype(F32)
    jj = lax.broadcasted_iota(I32, (ch, ch), 1).astype(F32)
    col = lax.broadcasted_iota(I32, (ch, 1), 0).astype(F32)
    lane = lax.broadcasted_iota(I32, (1, ch), 1).astype(F32)
    gn = gn_ref[0]

    consts = []
    for d in range(2):
        lg = lg_ref[d, h]
        diff = ii - jj if d == 0 else jj - ii
        dmat = jnp.where(diff >= 0, jnp.exp(lg * jnp.maximum(diff, 0.0)), 0.0) * kscale
        xi = jnp.exp(lg * (col + 1.0)) if d == 0 else jnp.exp(lg * (ch - col))
        zeta = (jnp.exp(lg * (ch - 1.0 - lane)) if d == 0 else jnp.exp(lg * lane)) * kscale
        cdec = jnp.exp(lg * jnp.full((1, 1), float(ch), F32))
        consts.append((dmat, xi, zeta, cdec))
        if has_init:
            s_scr[d] = s0_refs[d][0, 0, 0]
        else:
            s_scr[d] = jnp.zeros(s_scr.shape[1:], F32)

    def chunk_out(d, c):
        dmat, xi, zeta, cdec = consts[d]
        r0 = pl.multiple_of(c * ch, ch)
        qc = q_ref[0, pl.ds(r0, ch), :]
        ktc = kt_ref[:, pl.ds(r0, ch)]
        vc = v_ref[0, pl.ds(r0, ch), :]
        sc = _bdot(qc, ktc) * dmat
        s = s_scr[d]
        o = _bdot(sc.astype(BF16), vc) + _bdot(qc, s.astype(BF16)) * xi
        kz = (ktc.astype(F32) * zeta).astype(BF16)
        s_scr[d] = s * cdec + _bdot(kz, vc)
        return r0, o

    def finalize(r0, o):
        mu = jnp.mean(o, axis=-1, keepdims=True)
        var = jnp.mean(jnp.square(o - mu), axis=-1, keepdims=True)
        on = (o - mu) * lax.rsqrt(var + EPS) * gn
        gate = _silu(g_ref[0, pl.ds(r0, ch), :].astype(F32))
        y_ref[0, pl.ds(r0, ch), :] = (gate * on).astype(y_ref.dtype)

    def first_half(n, carry):
        for d in range(2):
            r0, o = chunk_out(d, n if d == 0 else nc - 1 - n)
            o_scr[pl.ds(r0, ch), :] = o
        return carry

    def second_half(n, carry):
        for d in range(2):
            r0, o = chunk_out(d, n if d == 0 else nc - 1 - n)
            finalize(r0, o_scr[pl.ds(r0, ch), :] + o)
        return carry

    lax.fori_loop(0, nc // 2, first_half, 0)
    lax.fori_loop(nc // 2, nc, second_half, 0)
    if emit_state:
        for d in range(2):
            so_refs[d][0, 0, 0] = s_scr[d]


def retention(q, kt, vg, row0, nb, seq, n_heads, dk, dv, log_gamma, gn, *, s_f0=None, s_b0=None, emit_state=False):
    t = q.shape[0]
    q3 = q.reshape(t // seq, seq, n_heads * dk)
    vg3 = vg.reshape(t // seq, seq, 2 * n_heads * dv)
    b0 = row0 // seq
    has_init = s_f0 is not None
    specs = [pl.BlockSpec(memory_space=pltpu.SMEM),
             pl.BlockSpec((1, seq, dk), lambda b, h: (b0 + b, 0, h)),
             pl.BlockSpec((dk, seq), lambda b, h: (h, b0 + b)),
             pl.BlockSpec((1, seq, dv), lambda b, h: (b0 + b, 0, h)),
             pl.BlockSpec((1, seq, dv), lambda b, h: (b0 + b, 0, n_heads + h)),
             pl.BlockSpec((1, 1, dv), lambda b, h: (h, 0, 0))]
    args = [log_gamma, q3, kt, vg3, vg3, gn.reshape(n_heads, 1, dv)]
    st = pl.BlockSpec((1, 1, 1, dk, dv), lambda b, h: (b, 0, h, 0, 0))
    if has_init:
        specs += [st, st]
        args += [s_f0, s_b0]
    out_specs = [pl.BlockSpec((1, seq, dv), lambda b, h: (b, 0, h))]
    out_shape = [jax.ShapeDtypeStruct((nb, seq, n_heads * dv), BF16)]
    if emit_state:
        out_specs += [st, st]
        out_shape += [jax.ShapeDtypeStruct((nb, 1, n_heads, dk, dv), F32)] * 2
    gs = pl.GridSpec(grid=(nb, n_heads), in_specs=specs, out_specs=out_specs,
                     scratch_shapes=[pltpu.VMEM((seq, dv), F32), pltpu.VMEM((2, dk, dv), F32)])
    ch = 2 * RET_CHUNK if seq % (4 * RET_CHUNK) == 0 else RET_CHUNK
    outs = pl.pallas_call(
        functools.partial(_ret_kernel, ch=ch, nc=seq // ch, dk=dk, has_init=has_init, emit_state=emit_state),
        grid_spec=gs, out_shape=out_shape, name="retention",
        compiler_params=pltpu.CompilerParams(dimension_semantics=("arbitrary", "arbitrary"),
                                             vmem_limit_bytes=VMEM_LIMIT))(*args)
    y = outs[0].reshape(nb * seq, n_heads * dv)
    return (y,) + tuple(outs[1:])


def _ffn_up_kernel(idx_ref, h_hbm, w1_ref, w3_ref, o_ref, xf_scr, xb_scr, sem, *, tm, nf):
    tile = pl.program_id(0)
    f = pl.program_id(1)
    slot = tile % 2
    per = tm // nf

    def row_copy(row, dst_slot, r):
        return pltpu.make_async_copy(h_hbm.at[pl.ds(row, 1)], xf_scr.at[dst_slot, pl.ds(r, 1)], sem.at[dst_slot])

    def wait_rows(src_slot):
        pltpu.make_async_copy(h_hbm.at[pl.ds(0, tm)], xf_scr.at[src_slot], sem.at[src_slot]).wait()

    @pl.when(jnp.logical_and(tile == 0, f == 0))
    def _():
        def issue(r, carry):
            row_copy(idx_ref[r], 0, r).start()
            return carry

        lax.fori_loop(0, tm, issue, 0)

    @pl.when(f == 0)
    def _():
        wait_rows(slot)
        xb_scr[...] = xf_scr[slot].astype(BF16)

    base = (tile + 1) * tm + f * per
    for r in range(per):
        row_copy(idx_ref[base + r], 1 - slot, f * per + r).start()

    x = xb_scr[...]
    a = _bdot(x, w1_ref[0, 0].astype(BF16))
    b = _bdot(x, w3_ref[0, 0].astype(BF16))
    o_ref[...] = (_silu(a) * b).astype(o_ref.dtype)

    @pl.when(jnp.logical_and(tile == pl.num_programs(0) - 1, f == nf - 1))
    def _():
        wait_rows(1 - slot)


def ffn_up(h, rows, w1, w3, layer, tm, tiles_per_expert):
    _, n_exp, d, ff = w1.shape
    nslots = rows.shape[0]
    tf = _pick(ff, (512, 256, 128))
    nf = ff // tf
    tpe = tiles_per_expert
    rows = jnp.pad(rows, (0, tm))
    wspec = pl.BlockSpec((1, 1, d, tf), lambda t, f, idx: (layer, t // tpe, 0, f))
    return _call(functools.partial(_ffn_up_kernel, tm=tm, nf=nf), grid=(nslots // tm, nf), nsp=1,
                 in_specs=[pl.BlockSpec(memory_space=pl.ANY), wspec, wspec],
                 out_specs=pl.BlockSpec((tm, tf), lambda t, f, idx: (t, f)),
                 out_shape=jax.ShapeDtypeStruct((nslots, ff), BF16),
                 scratch=[pltpu.VMEM((2, tm, d), F32), pltpu.VMEM((tm, d), BF16), pltpu.SemaphoreType.DMA((2,))],
                 name="ffn_up")(rows, h, w1, w3)


def _ffn_down_kernel(x_ref, w_ref, g_ref, o_ref, wb):
    @pl.when(pl.program_id(2) == 0)
    def _():
        wb[...] = w_ref[0, 0].astype(BF16)

    acc = _bdot(x_ref[...], wb[...])
    g = g_ref[...]
    for c in range(acc.shape[1] // LANE):
        o_ref[:, c * LANE:(c + 1) * LANE] = acc[:, c * LANE:(c + 1) * LANE] * g


def ffn_down(hm, w2, layer, gates, tm, tiles_per_expert):
    _, n_exp, ff, d = w2.shape
    nslots = hm.shape[0]
    tn = _pick(d, (512, 256, 128))
    tpe = tiles_per_expert
    return _call(_ffn_down_kernel, grid=(n_exp, d // tn, tpe),
                 in_specs=[pl.BlockSpec((tm, ff), lambda e, j, r: (e * tpe + r, 0)),
                           pl.BlockSpec((1, 1, ff, tn), lambda e, j, r: (layer, e, 0, j)),
                           pl.BlockSpec((tm, LANE), lambda e, j, r: (e * tpe + r, 0))],
                 out_specs=pl.BlockSpec((tm, tn), lambda e, j, r: (e * tpe + r, j)),
                 out_shape=jax.ShapeDtypeStruct((nslots, d), F32),
                 scratch=[pltpu.VMEM((ff, tn), BF16)], name="ffn_down")(hm, w2, gates)


SLOT_GROUP = 8
COMBINE_BATCH = 32


def _combine_kernel(flat_ref, pstart_ref, cidx_ref, y_hbm, t_hbm, x_ref, g_ref, *refs, nsplit):
    nout = 1 if nsplit is None else 2
    o_refs = refs[:nout]
    z_scr, tz_scr, acc_scr, st_ref, sem = refs[nout:]
    i = pl.program_id(0)
    tl = x_ref.shape[0]
    gb = COMBINE_BATCH
    rows_b = gb * SLOT_GROUP
    p0, p1 = pstart_ref[i], pstart_ref[i + 1]
    npos = p1 - p0
    nbatch = (npos + gb - 1) // gb
    next_has = jnp.logical_and(i + 1 < pl.num_programs(0), pstart_ref[i + 2] > p1)

    @pl.when(i == 0)
    def _():
        st_ref[0] = 0
        st_ref[1] = 0

    s0 = st_ref[0]
    started = st_ref[1]

    def gather(pbase, slot):
        def issue(k, carry):
            r = pl.multiple_of(flat_ref[pbase + k] * SLOT_GROUP, SLOT_GROUP)
            ko = pl.multiple_of(k * SLOT_GROUP, SLOT_GROUP)
            pltpu.make_async_copy(y_hbm.at[pl.ds(r, SLOT_GROUP)], z_scr.at[slot, pl.ds(ko, SLOT_GROUP)],
                                  sem.at[slot]).start()
            pltpu.make_async_copy(t_hbm.at[pl.ds(r, SLOT_GROUP)], tz_scr.at[slot, pl.ds(ko, SLOT_GROUP)],
                                  sem.at[slot]).start()
            return carry

        lax.fori_loop(0, gb, issue, 0, unroll=8)

    @pl.when(jnp.logical_and(nbatch > 0, started == 0))
    def _():
        gather(p0, s0)

    acc_scr[...] = jnp.zeros_like(acc_scr)
    t_col = (i * tl + lax.broadcasted_iota(I32, (tl, 1), 0)).astype(F32)
    lane_j = lax.broadcasted_iota(I32, (1, rows_b), 1)

    def batch(bi, carry):
        slot = (s0 + bi) % 2

        @pl.when(bi + 1 < nbatch)
        def _():
            gather(p0 + (bi + 1) * gb, 1 - slot)

        @pl.when(jnp.logical_and(bi + 1 == nbatch, next_has))
        def _():
            gather(p1, 1 - slot)

        pltpu.make_async_copy(y_hbm.at[pl.ds(0, rows_b)], z_scr.at[slot], sem.at[slot]).wait()
        pltpu.make_async_copy(t_hbm.at[pl.ds(0, rows_b)], tz_scr.at[slot], sem.at[slot]).wait()
        tok_row = tz_scr[slot].T[0:1, :]
        tok_row = jnp.where(lane_j < (npos - bi * gb) * SLOT_GROUP, tok_row, -1.0)
        onehot = jnp.where(tok_row == t_col, 1.0, 0.0).astype(BF16)
        zh, zl = _split(z_scr[slot])
        acc_scr[...] += _bdot(onehot, zh) + _bdot(onehot, zl)
        return carry

    lax.fori_loop(0, nbatch, batch, 0)

    @pl.when(nbatch > 0)
    def _():
        st_ref[0] = (s0 + nbatch) % 2
        st_ref[1] = next_has.astype(I32)

    @pl.when(nbatch == 0)
    def _():
        st_ref[1] = 0

    out = x_ref[...] + g_ref[0, 0] * acc_scr[...]
    if nsplit is None:
        o_refs[0][...] = out
    else:
        @pl.when(i < nsplit)
        def _():
            o_refs[0][...] = out

        @pl.when(i >= nsplit)
        def _():
            o_refs[1][...] = out


def moe_combine(x, y, toks, flat, pstart, mods, cidx, k_gate, split=None):
    t, d = x.shape
    tl = TOK_TILE
    blk = pl.BlockSpec((tl, d), lambda i, *_: (i, 0))
    if split is None:
        nsplit, out_specs, out_shape = None, blk, jax.ShapeDtypeStruct((t, d), F32)
    else:
        nsplit, n1 = split // tl, (t - split) // tl
        out_specs = [pl.BlockSpec((tl, d), lambda i, *_: (jnp.minimum(i, nsplit - 1), 0)),
                     pl.BlockSpec((tl, d), lambda i, *_: (jnp.clip(i - nsplit, 0, n1 - 1), 0))]
        out_shape = [jax.ShapeDtypeStruct((split, d), F32), jax.ShapeDtypeStruct((t - split, d), F32)]
    rows_b = COMBINE_BATCH * SLOT_GROUP
    return _call(functools.partial(_combine_kernel, nsplit=nsplit), grid=(t // tl,), nsp=3,
                 in_specs=[pl.BlockSpec(memory_space=pl.ANY), pl.BlockSpec(memory_space=pl.ANY), blk,
                           pl.BlockSpec((1, 1, 1, d), lambda i, f, p, c: (c[i], k_gate, 0, 0))],
                 out_specs=out_specs, out_shape=out_shape,
                 scratch=[pltpu.VMEM((2, rows_b, d), F32), pltpu.VMEM((2, rows_b, LANE), F32),
                          pltpu.VMEM((tl, d), F32), pltpu.SMEM((2,), I32), pltpu.SemaphoreType.DMA((2,))],
                 name="moe_combine")(flat, pstart, cidx, y, toks, x, mods)


def _route(aff_t, groups):
    n_exp = aff_t.shape[0]
    caps = [CAPACITY_FACTOR * seq // n_exp for _, _, seq in groups]
    slots = sum(nb * cap for (_, nb, _), cap in zip(groups, caps))
    e_ids = jnp.arange(n_exp, dtype=I32)[:, None, None]
    rows, gates, g0s, ngs = [], [], [], []
    off = 0
    for (row0, nb, seq), cap in zip(groups, caps):
        a = lax.slice_in_dim(aff_t, row0, row0 + nb * seq, axis=1).reshape(n_exp, nb, seq)
        gate, idx = lax.top_k(a, cap)
        idx, gate = lax.sort((idx.astype(I32), gate), dimension=2, num_keys=1)
        b_ids = jnp.arange(nb, dtype=I32)[None, :, None]
        rows.append((row0 + b_ids * seq + idx).reshape(n_exp, nb * cap))
        gates.append(gate.reshape(n_exp, nb * cap))
        ntile = seq // TOK_TILE
        bounds = jnp.arange(ntile + 1, dtype=I32) * TOK_TILE
        cnt = jnp.sum(idx[:, :, None, :] < bounds[None, None, :, None], axis=-1, dtype=I32)
        base = e_ids * slots + off + b_ids * cap
        lo, hi = base + cnt[..., :-1], base + cnt[..., 1:]
        g0 = lo // SLOT_GROUP
        ng = jnp.where(hi > lo, (hi + SLOT_GROUP - 1) // SLOT_GROUP - g0, 0)
        g0s.append(g0.transpose(1, 2, 0).reshape(nb * ntile, n_exp))
        ngs.append(ng.transpose(1, 2, 0).reshape(nb * ntile, n_exp))
        off += nb * cap
    g0 = jnp.concatenate(g0s, axis=0).reshape(-1)
    ng = jnp.concatenate(ngs, axis=0).reshape(-1)
    ntiles = g0.shape[0] // n_exp
    cum = jnp.cumsum(ng)
    start = cum - ng
    pstart = jnp.concatenate([start[::n_exp], cum[-1:], cum[-1:]]).astype(I32)
    pmax = n_exp * slots // SLOT_GROUP + 2 * ntiles * n_exp + COMBINE_BATCH
    p = jnp.arange(pmax, dtype=I32)
    delta = g0 - start
    step = delta - jnp.concatenate([jnp.zeros((1,), I32), delta[:-1]])
    offs = jnp.sum(jnp.where(p[:, None] >= start[None, :], step[None, :], 0), axis=1, dtype=I32)
    flat = jnp.where(p < cum[-1], p + offs, 0).astype(I32)
    return jnp.concatenate(rows, axis=1), jnp.concatenate(gates, axis=1), flat, pstart


def ec_moe(x, gain, mods, cidx, router, w1, w3, w2, layer, groups, split=None):
    n_exp = router.shape[1]
    h, aff_t = norm_mod(x, gain, mods, cidx, 3, 4, router_t=router.T)
    rows, gates, flat, pstart = _route(aff_t, groups)
    slots = rows.shape[1]
    tm = _pick(slots, (1024, 512, 256, 128, 64, 32, 16, 8))
    tpe = slots // tm
    rows_f = rows.reshape(-1)
    hm = ffn_up(h, rows_f, w1, w3, layer, tm, tpe)
    lanes = lambda v: jnp.broadcast_to(v.reshape(-1, 1), (n_exp * slots, LANE))
    y = ffn_down(hm, w2, layer, lanes(gates), tm, tpe)
    return moe_combine(x, y, lanes(rows_f.astype(F32)), flat, pstart, mods, cidx, 5, split)


def kernel(x_prompt, x_sample, cache_attn_k, cache_attn_v, state_ret_fwd, state_ret_bwd, c, c_ctx, ada_w, ada_b, norm1_g, norm2_g, ev_w_in, ev_w_out, hy_sconv_w, hy_sconv_b, hy_ffn_w1, hy_ffn_b1, hy_ffn_w2, hy_ffn_b2, hy_ffn_w3, hy_freq, hy_bias, at_q_norm, at_k_norm, at_sink, od_w_in, od_w_out, ret_decay, ret_gn, moe_router, moe_w1, moe_w3, moe_w2):
    bp, lp, d = x_prompt.shape
    bs, ls, _ = x_sample.shape
    depth = ada_w.shape[0]
    tp, ts = bp * lp, bs * ls
    t = tp + ts
    groups = ((0, bp, lp), (tp, bs, ls))
    a_width = hy_bias.shape[-1]
    hd = at_q_norm.shape[-1]
    n_heads = at_sink.shape[-1]
    n_kv = cache_attn_k.shape[3]
    c_heads, c_dv = ret_gn.shape[1], ret_gn.shape[2]
    c_dk = (od_w_in.shape[-1] - 2 * c_heads * c_dv) // (2 * c_heads)

    ntp, nts = tp // TOK_TILE, ts // TOK_TILE
    tps, tss = lp // TOK_TILE, ls // TOK_TILE
    tile = np.arange(ntp + nts)
    cidx = jnp.asarray(np.where(tile < ntp, 0, 1 + (tile - ntp) // tss), I32)
    in_seq = np.where(tile < ntp, tile % tps, (tile - ntp) % tss)
    seq_tiles = np.where(tile < ntp, tps, tss)
    first = jnp.asarray(in_seq == 0, I32)
    last = jnp.asarray(in_seq == seq_tiles - 1, I32)
    tm_mm = _pick(math.gcd(tp, ls), (1024, 512, 256))
    tm_big = _pick(t, (2048, 1024, 512, 256))

    cond = jnp.concatenate([c_ctx[None, :], c], axis=0)
    cond = jnp.pad(cond, ((0, (-cond.shape[0]) % 8), (0, 0)))
    mods_all = adaln_all(cond, ada_w, ada_b)
    x = (x_prompt.reshape(tp, d), x_sample.reshape(ts, d))

    new_k, new_v, new_sf, new_sb = [], [], [], []
    for l in range(depth):
        mods = mods_all[l].reshape(-1, 6, 1, d)
        j = l // 2
        h = norm_mod(x, norm1_g[l], mods, cidx, 0, 1)
        if l % 2 == 0:
            u = matmul([h], ev_w_in[j], out_dtype=F32, tm=tm_big)
            z, x0 = hy_pre(u, hy_sconv_w[j], hy_sconv_b[j], first, last, a_width)
            convs = []
            for row0, nb, seq in groups:
                taps = hyena_taps(seq, hy_ffn_w1[j], hy_ffn_b1[j], hy_ffn_w2[j], hy_ffn_b2[j],
                                  hy_ffn_w3[j], hy_freq[j], a_width)
                conv_fn = long_conv_direct if seq <= 512 else long_conv_two_stage
                convs.append(conv_fn(z, row0, nb, seq, taps))
            ya = hy_post(x0, z, hy_bias[j], convs[0], convs[1])
            cos_s, sin_s = _rope_tables(ls, hd)
            cos_t = jnp.concatenate([jnp.ones((tp, hd), F32), jnp.tile(cos_s, (bs, 1))], axis=0)
            sin_t = jnp.concatenate([jnp.zeros((tp, hd), F32), jnp.tile(sin_s, (bs, 1))], axis=0)
            q, k, v, k_normed = qkv_prep(u, 3 * a_width, n_heads, n_kv, hd, at_q_norm[j], at_k_norm[j], cos_t, sin_t)
            yb_p = attn_context(q, k, v, at_sink[j], 0, bp, lp, n_kv, hd)
            past = cache_attn_k.shape[2]
            ck = cache_attn_k[:, j].reshape(bs, past, n_kv * hd)
            cv = cache_attn_v[:, j].reshape(bs, past, n_kv * hd)
            yb_s = attn_latent(q, k, v, ck, cv, at_sink[j], tp, bs, ls, n_kv, hd)
            x = matmul([ya, (yb_p, yb_s)], ev_w_out[j], out_dtype=F32, tm=tm_mm, res=x, mods=mods, cidx=cidx, k_gate=2)
            kv0 = 3 * a_width + n_heads * hd
            new_k.append(k_normed[:tp].reshape(bp, 1, lp, n_kv, hd))
            new_v.append(u[:tp, kv0 + n_kv * hd:kv0 + 2 * n_kv * hd].reshape(bp, 1, lp, n_kv, hd))
        else:
            nqk = c_heads * c_dk
            w_in = od_w_in[j]
            q = matmul([h], w_in, out_dtype=BF16, tm=tm_big, ncols=nqk)
            kt = matmul_t(h, w_in[:, nqk:2 * nqk].T, out_dtype=BF16, tm=tm_big)
            vg = matmul([h], w_in, out_dtype=BF16, tm=tm_big, col0=2 * nqk)
            log_gamma = -jnp.exp(ret_decay[j].astype(F32))
            y_p, sf, sb = retention(q, kt, vg, 0, bp, lp, c_heads, c_dk, c_dv, log_gamma, ret_gn[j], emit_state=True)
            (y_s,) = retention(q, kt, vg, tp, bs, ls, c_heads, c_dk, c_dv, log_gamma, ret_gn[j],
                               s_f0=state_ret_fwd[:, j:j + 1], s_b0=state_ret_bwd[:, j:j + 1])
            x = matmul([(y_p, y_s)], od_w_out[j], out_dtype=F32, tm=min(tm_mm, 512), res=x, mods=mods, cidx=cidx,
                       k_gate=2)
            new_sf.append(sf)
            new_sb.append(sb)
        x = ec_moe(x, norm2_g[l], mods, cidx, moe_router[l], moe_w1, moe_w3, moe_w2, l, groups,
                   split=tp if l == depth - 1 else None)

    cat = lambda parts: jnp.concatenate(parts, axis=1)
    return (x[0].reshape(bp, lp, d), x[1].reshape(bs, ls, d), cat(new_k), cat(new_v), cat(new_sf), cat(new_sb))
```

```python
import functools
import math

import ml_dtypes
import numpy as np
import jax
import jax.numpy as jnp
from jax import lax
from jax.experimental import pallas as pl
from jax.experimental.pallas import tpu as pltpu

F32 = jnp.float32
BF16 = jnp.bfloat16
I32 = jnp.int32
HIGHEST = lax.Precision.HIGHEST

EPS = 1e-6
NEG_INF = -1e30
GRID_W = 64
WINDOW = 128
ATT_BLOCK = 128
RET_CHUNK = 128
ROPE_BASE = 10000.0
A_BANDS = 16
A_SHORT_DECAY_PCT = 0.3
A_LONG_DECAY_PCT = 1.5
A_DECAY_TARGET = 1e-2
CAPACITY_FACTOR = 2

TOK_TILE = 256
LANE = 128
FFT_N2 = 128
VMEM_LIMIT = 56 * 1024 * 1024


def _pick(n, cands):
    for c in cands:
        if n % c == 0:
            return c
    raise ValueError(f"no tile for {n} in {cands}")


def _call(kernel, *, grid, in_specs, out_specs, out_shape, scratch=(), nsp=0, name=None):
    gs = pltpu.PrefetchScalarGridSpec(num_scalar_prefetch=nsp, grid=grid, in_specs=in_specs,
                                      out_specs=out_specs, scratch_shapes=list(scratch))
    cp = pltpu.CompilerParams(dimension_semantics=("arbitrary",) * len(grid), vmem_limit_bytes=VMEM_LIMIT)
    return pl.pallas_call(kernel, grid_spec=gs, out_shape=out_shape, compiler_params=cp, name=name)


def _silu(x):
    return x / (1.0 + jnp.exp(-x))


def _bdot(a, b):
    return jnp.dot(a, b, preferred_element_type=F32)


def _split(x):
    hi = x.astype(BF16)
    lo = (x - hi.astype(F32)).astype(BF16)
    return hi, lo


def _dot3(mh, ml, x):
    xh, xl = _split(x)
    return _bdot(mh, xh) + _bdot(mh, xl) + _bdot(ml, xh)


def _adaln_kernel(c_ref, w_ref, b_ref, o_ref):
    s = _silu(c_ref[...])
    o_ref[0] = _bdot(s.astype(BF16), w_ref[0].astype(BF16)) + b_ref[0]


def adaln_all(cond, ada_w, ada_b):
    depth, d, n6 = ada_w.shape
    rows = cond.shape[0]
    tn = _pick(n6, (1024, 512, 256, 128))
    return _call(
        _adaln_kernel, grid=(depth, n6 // tn),
        in_specs=[pl.BlockSpec((rows, d), lambda l, j: (0, 0)),
                  pl.BlockSpec((1, d, tn), lambda l, j: (l, 0, j)),
                  pl.BlockSpec((1, 1, tn), lambda l, j: (l, 0, j))],
        out_specs=pl.BlockSpec((1, rows, tn), lambda l, j: (l, 0, j)),
        out_shape=jax.ShapeDtypeStruct((depth, rows, n6), F32), name="adaln",
    )(cond, ada_w, ada_b.reshape(depth, 1, n6))


def _row_sources(a, tm):
    if not isinstance(a, tuple):
        return [a], [lambda i: i], None
    a0, a1 = a
    n0, n1 = a0.shape[0] // tm, a1.shape[0] // tm
    return [a0, a1], [lambda i: jnp.minimum(i, n0 - 1), lambda i: jnp.clip(i - n0, 0, n1 - 1)], n0


def _load_rows(refs, i, split):
    if split is None:
        return refs[0][...]
    return jnp.where(i < split, refs[0][...], refs[1][...])


def _norm_mod(x, g_ref, sc_ref, sh_ref):
    ms = jnp.mean(x * x, axis=-1, keepdims=True)
    y = x * lax.rsqrt(ms + EPS) * g_ref[...]
    return y * (1.0 + sc_ref[0, 0]) + sh_ref[0, 0]


def _norm_mod_kernel(cidx_ref, *refs, nsrc, split):
    g_ref, sc_ref, sh_ref, o_ref = refs[nsrc:]
    x = _load_rows(refs[:nsrc], pl.program_id(0), split)
    o_ref[...] = _norm_mod(x, g_ref, sc_ref, sh_ref).astype(o_ref.dtype)


def _norm_router_kernel(cidx_ref, *refs, nsrc, split):
    g_ref, sc_ref, sh_ref, rt_ref, h_ref, aff_ref = refs[nsrc:]
    x = _load_rows(refs[:nsrc], pl.program_id(0), split)
    h = _norm_mod(x, g_ref, sc_ref, sh_ref)
    h_ref[...] = h
    nt = lambda a, b: lax.dot_general(a, b, (((1,), (1,)), ((), ())), preferred_element_type=F32)
    rh, rl = _split(rt_ref[...])
    hh, hl = _split(h)
    logits = nt(rh, hh) + nt(rh, hl) + nt(rl, hh)
    m = jnp.max(logits, axis=0, keepdims=True)
    e = jnp.exp(logits - m)
    aff_ref[...] = e / jnp.sum(e, axis=0, keepdims=True)


def norm_mod(x, gain, mods, cidx, k_shift, k_scale, *, router_t=None):
    tm = TOK_TILE
    srcs, fns, split = _row_sources(x, tm)
    t = sum(a.shape[0] for a in srcs)
    d = srcs[0].shape[1]
    specs = [pl.BlockSpec((tm, d), lambda i, c, f=f: (f(i), 0)) for f in fns]
    specs += [pl.BlockSpec((1, d), lambda i, c: (0, 0)),
              pl.BlockSpec((1, 1, 1, d), lambda i, c: (c[i], k_scale, 0, 0)),
              pl.BlockSpec((1, 1, 1, d), lambda i, c: (c[i], k_shift, 0, 0))]
    kw = dict(nsrc=len(srcs), split=split)
    if router_t is None:
        return _call(functools.partial(_norm_mod_kernel, **kw), grid=(t // tm,), nsp=1, in_specs=specs,
                     out_specs=pl.BlockSpec((tm, d), lambda i, c: (i, 0)),
                     out_shape=jax.ShapeDtypeStruct((t, d), BF16), name="norm_mod",
                     )(cidx, *srcs, gain.reshape(1, d), mods, mods)
    e = router_t.shape[0]
    specs.append(pl.BlockSpec((e, d), lambda i, c: (0, 0)))
    return _call(functools.partial(_norm_router_kernel, **kw), grid=(t // tm,), nsp=1, in_specs=specs,
                 out_specs=[pl.BlockSpec((tm, d), lambda i, c: (i, 0)),
                            pl.BlockSpec((e, tm), lambda i, c: (0, i))],
                 out_shape=[jax.ShapeDtypeStruct((t, d), F32), jax.ShapeDtypeStruct((e, t), F32)],
                 name="norm_router",
                 )(cidx, *srcs, gain.reshape(1, d), mods, mods, router_t)


def _mm_kernel(*refs, xcounts, xsplits, gated, rcount, rsplit):
    pos = 1 if gated else 0
    i = pl.program_id(1)
    x_refs = []
    for cnt in xcounts:
        x_refs.append(refs[pos:pos + cnt])
        pos += cnt
    nx = len(xcounts)
    w_refs = refs[pos:pos + nx]
    pos += nx
    if gated:
        res_refs, gate_ref = refs[pos:pos + rcount], refs[pos + rcount]
        pos += rcount + 1
    o_ref = refs[pos]
    wb_refs = refs[pos + 1:]

    @pl.when(i == 0)
    def _():
        for w_ref, wb in zip(w_refs, wb_refs):
            wb[...] = w_ref[...].astype(BF16)

    acc = None
    for xr, split, wb in zip(x_refs, xsplits, wb_refs):
        part = _bdot(_load_rows(xr, i, split), wb[...])
        acc = part if acc is None else acc + part
    if gated:
        acc = _load_rows(res_refs, i, rsplit) + gate_ref[0, 0] * acc
    o_ref[...] = acc.astype(o_ref.dtype)


def matmul(xs, w, *, out_dtype, tm, tn=512, col0=0, ncols=None, res=None, mods=None, cidx=None, k_gate=None):
    srcs = [_row_sources(x, tm) for x in xs]
    m = sum(a.shape[0] for a in srcs[0][0])
    kk = srcs[0][0][0].shape[1]
    nx = len(xs)
    assert w.shape[0] == nx * kk
    n = w.shape[1] - col0 if ncols is None else ncols
    tn = _pick(math.gcd(n, col0) if col0 else n, (tn, 512, 256, 128))
    cb0 = col0 // tn
    gated = res is not None
    if gated:
        imap = lambda f: (lambda j, i, c: f(j, i, c))
    else:
        imap = lambda f: (lambda j, i: f(j, i, None))
    specs, args = [], []
    for arrs, fns, _ in srcs:
        specs += [pl.BlockSpec((tm, kk), imap(lambda j, i, c, f=f: (f(i), 0))) for f in fns]
        args += arrs
    specs += [pl.BlockSpec((kk, tn), imap(lambda j, i, c, q=q: (q, cb0 + j))) for q in range(nx)]
    args += [w] * nx
    rcount, rsplit = 0, None
    if gated:
        assert mods.shape[-1] == n and col0 == 0
        rarrs, rfns, rsplit = _row_sources(res, tm)
        rcount = len(rarrs)
        specs += [pl.BlockSpec((tm, tn), imap(lambda j, i, c, f=f: (f(i), j))) for f in rfns]
        stride = tm // TOK_TILE
        specs.append(pl.BlockSpec((1, 1, 1, tn), imap(lambda j, i, c: (c[i * stride], k_gate, 0, j))))
        args = [cidx] + args + rarrs + [mods]
    kern = functools.partial(_mm_kernel, xcounts=tuple(len(s[0]) for s in srcs),
                             xsplits=tuple(s[2] for s in srcs), gated=gated, rcount=rcount, rsplit=rsplit)
    return _call(kern, grid=(n // tn, m // tm), nsp=1 if gated else 0, in_specs=specs,
                 out_specs=pl.BlockSpec((tm, tn), imap(lambda j, i, c: (i, j))),
                 out_shape=jax.ShapeDtypeStruct((m, n), out_dtype),
                 scratch=[pltpu.VMEM((kk, tn), BF16) for _ in xs], name="matmul",
                 )(*args)


def _mm_t_kernel(x_ref, w_ref, o_ref, wb):
    @pl.when(pl.program_id(1) == 0)
    def _():
        wb[...] = w_ref[...].astype(BF16)

    o_ref[...] = lax.dot_general(wb[...], x_ref[...], (((1,), (1,)), ((), ())),
                                 preferred_element_type=F32).astype(o_ref.dtype)


def matmul_t(x, wt, *, out_dtype, tm, tn=512):
    m, kk = x.shape
    n = wt.shape[0]
    tn = _pick(n, (tn, 512, 256, 128))
    return _call(_mm_t_kernel, grid=(n // tn, m // tm),
                 in_specs=[pl.BlockSpec((tm, kk), lambda j, i: (i, 0)), pl.BlockSpec((tn, kk), lambda j, i: (j, 0))],
                 out_specs=pl.BlockSpec((tn, tm), lambda j, i: (j, i)),
                 out_shape=jax.ShapeDtypeStruct((n, m), out_dtype),
                 scratch=[pltpu.VMEM((tn, kk), BF16)], name="matmul_t")(x, wt)


def _hy_pre_kernel(first_ref, last_ref, *refs):
    ins = refs[:15]
    z_ref, x0_ref = refs[15], refs[16]
    i = pl.program_id(0)
    is_first = first_ref[i] == 1
    is_last = last_ref[i] == 1

    def sconv(m_ref, p_ref, n_ref, w_ref, b_ref):
        x = m_ref[...]
        tl = x.shape[0]
        rows = lax.broadcasted_iota(I32, x.shape, 0)
        prev_row = jnp.where(is_first, 0.0, p_ref[7:8, :])
        next_row = jnp.where(is_last, 0.0, n_ref[0:1, :])
        xm1 = jnp.where(rows == 0, prev_row, pltpu.roll(x, 1, 0))
        xp1 = jnp.where(rows == tl - 1, next_row, pltpu.roll(x, tl - 1, 0))
        w = w_ref[...]
        return xm1 * w[0:1] + x * w[1:2] + xp1 * w[2:3] + b_ref[...]

    hv = sconv(ins[0], ins[1], ins[2], ins[9], ins[12])
    x0 = sconv(ins[3], ins[4], ins[5], ins[10], ins[13])
    x1 = sconv(ins[6], ins[7], ins[8], ins[11], ins[14])
    z_ref[...] = x1 * hv
    x0_ref[...] = x0


def hy_pre(u, sc_w, sc_b, first, last, a_width):
    t = u.shape[0]
    tl = TOK_TILE
    cb = _pick(a_width, (512, 256, 128))
    ncb = a_width // cb
    nt8 = t // 8
    specs = []
    for part in range(3):
        off = part * ncb
        specs += [pl.BlockSpec((tl, cb), lambda i, j, f, l, off=off: (i, off + j)),
                  pl.BlockSpec((8, cb), lambda i, j, f, l, off=off: (jnp.maximum(i * (tl // 8) - 1, 0), off + j)),
                  pl.BlockSpec((8, cb), lambda i, j, f, l, off=off: (jnp.minimum((i + 1) * (tl // 8), nt8 - 1), off + j))]
    for part in range(3):
        off = part * ncb
        specs.append(pl.BlockSpec((3, cb), lambda i, j, f, l, off=off: (0, off + j)))
    for part in range(3):
        off = part * ncb
        specs.append(pl.BlockSpec((1, cb), lambda i, j, f, l, off=off: (0, off + j)))
    out_spec = pl.BlockSpec((tl, cb), lambda i, j, f, l: (i, j))
    scb = sc_b.reshape(1, -1)
    return _call(_hy_pre_kernel, grid=(t // tl, ncb), nsp=2, in_specs=specs,
                 out_specs=[out_spec, out_spec],
                 out_shape=[jax.ShapeDtypeStruct((t, a_width), F32)] * 2, name="hy_pre",
                 )(first, last, *([u] * 9), sc_w, sc_w, sc_w, scb, scb, scb)


def _filter_kernel(fa_ref, fb_ref, w1_ref, b1_ref, w2_ref, b2_ref, fr_ref, w3f_ref, w3b_ref, dl_ref, o_ref):
    hdot = functools.partial(jnp.dot, precision=HIGHEST, preferred_element_type=F32)

    def taps(feats, w3_ref):
        h = jnp.sin(fr_ref[0:1, :] * (hdot(feats, w1_ref[...]) + b1_ref[...]))
        h = jnp.sin(fr_ref[1:2, :] * (hdot(h, w2_ref[...]) + b2_ref[...]))
        return hdot(h, w3_ref[...]) * jnp.exp(-feats[:, 0:1] * dl_ref[...])

    hf = taps(fa_ref[...], w3f_ref)
    hb = taps(fb_ref[...], w3b_ref)
    s = (jnp.sum(jnp.abs(hf), axis=0, keepdims=True) + jnp.sum(jnp.abs(hb), axis=0, keepdims=True)) + EPS
    o_ref[0] = hf / s
    o_ref[1] = jnp.where(lax.broadcasted_iota(I32, hb.shape, 0) == 0, 0.0, hb / s)


def hyena_taps(seq, w1, b1, w2, b2, w3, freq, a_width):
    t = jnp.arange(seq, dtype=F32) / seq
    bands = jnp.linspace(1e-4, A_BANDS - 1, A_BANDS, dtype=F32)
    ang = 2.0 * math.pi * t[:, None] * bands[None, :]
    feats = jnp.concatenate([t[:, None], jnp.cos(ang), -jnp.sin(ang)], axis=-1)
    emb = feats.shape[1]
    feats = jnp.pad(feats, ((0, 0), (0, LANE - emb)))
    feats_b = jnp.roll(feats[::-1], 1, axis=0)
    w1p = jnp.pad(w1, ((0, LANE - emb), (0, 0)))
    max_decay = math.log(A_DECAY_TARGET) / A_SHORT_DECAY_PCT
    min_decay = math.log(A_DECAY_TARGET) / A_LONG_DECAY_PCT
    deltas = jnp.abs(jnp.linspace(min_decay, max_decay, a_width, dtype=F32)).reshape(1, a_width)
    ffn = w2.shape[0]
    cb = _pick(a_width, (256, 128))
    ncb = a_width // cb
    full = lambda shape: pl.BlockSpec(shape, lambda j: (0,) * len(shape))
    taps = _call(_filter_kernel, grid=(ncb,),
                 in_specs=[full((seq, LANE)), full((seq, LANE)), full((LANE, ffn)), full((1, ffn)), full((ffn, ffn)),
                           full((1, ffn)), full((2, ffn)),
                           pl.BlockSpec((ffn, cb), lambda j: (0, j)),
                           pl.BlockSpec((ffn, cb), lambda j: (0, ncb + j)),
                           pl.BlockSpec((1, cb), lambda j: (0, j))],
                 out_specs=pl.BlockSpec((2, seq, cb), lambda j: (0, 0, j)),
                 out_shape=jax.ShapeDtypeStruct((2, seq, a_width), F32), name="hy_filter",
                 )(feats, feats_b, w1p, b1.reshape(1, ffn), w2, b2.reshape(1, ffn), freq, w3, w3, deltas)
    return taps.reshape(2 * seq, a_width)


def _hilo(x):
    x32 = np.asarray(x, np.float32)
    hi = x32.astype(ml_dtypes.bfloat16)
    lo = (x32 - hi.astype(np.float32)).astype(ml_dtypes.bfloat16)
    return hi, lo


@functools.lru_cache(maxsize=None)
def _dft_direct_consts(seq):
    n = 2 * seq
    k = np.arange(n)
    ang = 2.0 * np.pi * ((k[:, None] * k[None, :]) % n) / n
    c, s = np.cos(ang), np.sin(ang)
    fwd = np.block([[c[:, :seq], s[:, :seq]], [-s[:, :seq], c[:, :seq]]])
    taps = np.concatenate([c, -s], axis=0)
    inv = np.block([[c[:seq, :], -s[:seq, :]], [s[:seq, :], c[:seq, :]]]) / n
    return _hilo(fwd), _hilo(taps), _hilo(inv)


@functools.lru_cache(maxsize=None)
def _dft_two_stage_consts(seq):
    n = 2 * seq
    n2c = FFT_N2
    n1c = n // n2c
    h = n1c // 2
    n2 = np.arange(n2c)[:, None, None]
    k1 = np.arange(n1c)[None, :, None]
    n1 = np.arange(n1c)[None, None, :]
    psi = 2.0 * np.pi * ((n1 * k1 * n2c + n2 * k1) % n) / n
    c, s = np.cos(psi), np.sin(psi)
    m1d = np.concatenate([np.concatenate([c[:, :, :h], s[:, :, :h]], axis=2),
                          np.concatenate([-s[:, :, :h], c[:, :, :h]], axis=2)], axis=1)
    m1t = np.concatenate([c, -s], axis=1)
    ct = np.swapaxes(c, 1, 2)[:, :h, :]
    st = np.swapaxes(s, 1, 2)[:, :h, :]
    m1i = np.concatenate([np.concatenate([ct, -st], axis=2),
                          np.concatenate([st, ct], axis=2)], axis=1) / n
    kk = np.arange(n2c)
    th = 2.0 * np.pi * ((kk[:, None] * kk[None, :]) % n2c) / n2c
    c2, s2 = np.cos(th), np.sin(th)
    m2 = np.block([[c2, s2], [-s2, c2]])
    m2i = np.block([[c2, -s2], [s2, c2]])
    return _hilo(m1d), _hilo(m1t), _hilo(m1i), _hilo(m2), _hilo(m2i)


def _cmul_stacked(x, hs, half):
    xr, xi = x[:half], x[half:]
    hr, hi = hs[:half], hs[half:]
    return jnp.concatenate([xr * hr - xi * hi, xr * hi + xi * hr], axis=0)


def _const_mm3_kernel(mh_ref, ml_ref, x_ref, o_ref):
    o_ref[...] = _dot3(mh_ref[...], ml_ref[...], x_ref[...])


def const_mm3(mh, ml, x):
    m, k = mh.shape
    c = x.shape[1]
    cb = _pick(c, (256, 128))
    return _call(_const_mm3_kernel, grid=(c // cb,),
                 in_specs=[pl.BlockSpec((m, k), lambda j: (0, 0)), pl.BlockSpec((m, k), lambda j: (0, 0)),
                           pl.BlockSpec((k, cb), lambda j: (0, j))],
                 out_specs=pl.BlockSpec((m, cb), lambda j: (0, j)),
                 out_shape=jax.ShapeDtypeStruct((m, c), F32), name="dft_taps")(mh, ml, x)


def _conv_direct_kernel(fh_ref, fl_ref, gh_ref, gl_ref, zr_ref, zi_ref, h_ref, yr_ref, yi_ref, *, seq):
    x = jnp.concatenate([zr_ref[...], zi_ref[...]], axis=0)
    spec = _dot3(fh_ref[...], fl_ref[...], x)
    y = _dot3(gh_ref[...], gl_ref[...], _cmul_stacked(spec, h_ref[...], 2 * seq))
    yr_ref[...] = y[:seq]
    yi_ref[...] = y[seq:]


def long_conv_direct(z, row0, nb, seq, taps):
    a_width = z.shape[1]
    (fh, fl), (th, tl_), (gh, gl) = _dft_direct_consts(seq)
    spec_h = const_mm3(jnp.asarray(th), jnp.asarray(tl_), taps)
    cb = _pick(a_width, (256, 128))
    half = nb // 2
    b0 = row0 // seq
    full = lambda a: pl.BlockSpec(a.shape, lambda j, p: (0, 0))
    yr, yi = _call(
        functools.partial(_conv_direct_kernel, seq=seq), grid=(a_width // cb, half),
        in_specs=[full(fh), full(fl), full(gh), full(gl),
                  pl.BlockSpec((seq, cb), lambda j, p: (b0 + p, j)),
                  pl.BlockSpec((seq, cb), lambda j, p: (b0 + half + p, j)),
                  pl.BlockSpec((4 * seq, cb), lambda j, p: (0, j))],
        out_specs=[pl.BlockSpec((seq, cb), lambda j, p: (p, j))] * 2,
        out_shape=[jax.ShapeDtypeStruct((half * seq, a_width), F32)] * 2, name="conv_direct",
    )(jnp.asarray(fh), jnp.asarray(fl), jnp.asarray(gh), jnp.asarray(gl), z, z, spec_h)
    return jnp.concatenate([yr, yi], axis=0)


def _fft_s1_kernel(mh_ref, ml_ref, *refs, g, packed):
    o_ref = refs[-1]
    for j in range(g):
        if packed:
            x = jnp.concatenate([refs[0][0, j], refs[1][0, j]], axis=0)
        else:
            x = refs[0][0, j]
        o_ref[0, j] = _dot3(mh_ref[j], ml_ref[j], x)


def _fft_s1(mh, ml, zp, *, packed):
    n2c, m_rows, n1c = mh.shape
    bsz, _, rows, c = zp.shape
    p = bsz // 2 if packed else bsz
    cb = _pick(c, (256, 128))
    g = 16
    cspec = pl.BlockSpec((g, m_rows, n1c), lambda q, j, t: (t, 0, 0))
    zspecs = [pl.BlockSpec((1, g, rows, cb), lambda q, j, t: (q, t, 0, j))]
    args = [zp]
    if packed:
        zspecs.append(pl.BlockSpec((1, g, rows, cb), lambda q, j, t: (q + p, t, 0, j)))
        args.append(zp)
    return _call(functools.partial(_fft_s1_kernel, g=g, packed=packed), grid=(p, c // cb, n2c // g),
                 in_specs=[cspec, cspec] + zspecs,
                 out_specs=pl.BlockSpec((1, g, m_rows, cb), lambda q, j, t: (q, t, 0, j)),
                 out_shape=jax.ShapeDtypeStruct((p, n2c, m_rows, c), F32), name="fft_s1",
                 )(jnp.asarray(mh), jnp.asarray(ml), *args)


def _fft_s2_kernel(mh_ref, ml_ref, ih_ref, il_ref, *refs, g, conv):
    a_ref, o_ref = refs[0], refs[-1]
    half = a_ref.shape[2] // 2
    for j in range(g):
        spec = _dot3(mh_ref[...], ml_ref[...], a_ref[0, j])
        if conv:
            spec = _dot3(ih_ref[...], il_ref[...], _cmul_stacked(spec, refs[1][0, j], half))
        o_ref[0, j] = spec


def _fft_s2(m2, m2i, at, spec_h=None):
    p, n1c, rows, c = at.shape
    cb = _pick(c, (256, 128))
    g = _pick(n1c, (8, 4, 2, 1))
    conv = spec_h is not None
    mspec = pl.BlockSpec((rows, rows), lambda q, j, t: (0, 0))
    blk = pl.BlockSpec((1, g, rows, cb), lambda q, j, t: (q, t, 0, j))
    specs = [mspec] * 4 + [blk]
    args = [jnp.asarray(m2[0]), jnp.asarray(m2[1]), jnp.asarray(m2i[0]), jnp.asarray(m2i[1]), at]
    if conv:
        specs.append(pl.BlockSpec((1, g, rows, cb), lambda q, j, t: (0, t, 0, j)))
        args.append(spec_h)
    return _call(functools.partial(_fft_s2_kernel, g=g, conv=conv), grid=(p, c // cb, n1c // g),
                 in_specs=specs, out_specs=blk, out_shape=jax.ShapeDtypeStruct(at.shape, F32), name="fft_s2",
                 )(*args)


def _fft_s3_kernel(mh_ref, ml_ref, b_ref, yr_ref, yi_ref, *, g):
    half = yr_ref.shape[2]
    for j in range(g):
        y = _dot3(mh_ref[j], ml_ref[j], b_ref[0, j])
        yr_ref[0, j] = y[:half]
        yi_ref[0, j] = y[half:]


def _fft_s3(mh, ml, bt):
    n2c, n1c, rows = mh.shape
    p, _, _, c = bt.shape
    cb = _pick(c, (256, 128))
    g = 16
    h = n1c // 2
    oblk = pl.BlockSpec((1, g, h, cb), lambda q, j, t: (q, t, 0, j))
    cspec = pl.BlockSpec((g, n1c, rows), lambda q, j, t: (t, 0, 0))
    return _call(functools.partial(_fft_s3_kernel, g=g), grid=(p, c // cb, n2c // g),
                 in_specs=[cspec, cspec, pl.BlockSpec((1, g, rows, cb), lambda q, j, t: (q, t, 0, j))],
                 out_specs=[oblk, oblk],
                 out_shape=[jax.ShapeDtypeStruct((p, n2c, h, c), F32)] * 2, name="fft_s3",
                 )(jnp.asarray(mh), jnp.asarray(ml), bt)


def _swap_digits(a):
    p, x, y2, c = a.shape
    y = y2 // 2
    return a.reshape(p, x, 2, y, c).transpose(0, 3, 2, 1, 4).reshape(p, y, 2 * x, c)


def long_conv_two_stage(z, row0, nb, seq, taps):
    a_width = z.shape[1]
    m1d, m1t, m1i, m2, m2i = _dft_two_stage_consts(seq)
    n2c = FFT_N2
    n1c = 2 * seq // n2c
    taps_p = taps.reshape(1, n1c, n2c, a_width).transpose(0, 2, 1, 3)
    spec_h = _fft_s2(m2, m2i, _swap_digits(_fft_s1(m1t[0], m1t[1], taps_p, packed=False)))
    zs = lax.slice_in_dim(z, row0, row0 + nb * seq, axis=0)
    zp = zs.reshape(nb, n1c // 2, n2c, a_width).transpose(0, 2, 1, 3)
    a = _swap_digits(_fft_s1(m1d[0], m1d[1], zp, packed=True))
    b = _swap_digits(_fft_s2(m2, m2i, a, spec_h))
    yr, yi = _fft_s3(m1i[0], m1i[1], b)
    y = jnp.concatenate([yr, yi], axis=0)
    return y.transpose(0, 2, 1, 3).reshape(nb * seq, a_width)


def _hy_post_kernel(x0_ref, z_ref, bias_ref, cp_ref, cs_ref, o_ref, *, ntp):
    conv = jnp.where(pl.program_id(0) < ntp, cp_ref[...], cs_ref[...])
    o_ref[...] = (x0_ref[...] * (conv + z_ref[...] * bias_ref[...])).astype(o_ref.dtype)


def hy_post(x0, z, bias, conv_p, conv_s):
    t, a_width = x0.shape
    tl = TOK_TILE
    ntp = conv_p.shape[0] // tl
    nts = conv_s.shape[0] // tl
    cb = _pick(a_width, (512, 256, 128))
    blk = pl.BlockSpec((tl, cb), lambda i, j: (i, j))
    return _call(functools.partial(_hy_post_kernel, ntp=ntp), grid=(t // tl, a_width // cb),
                 in_specs=[blk, blk, pl.BlockSpec((1, cb), lambda i, j: (0, j)),
                           pl.BlockSpec((tl, cb), lambda i, j: (jnp.minimum(i, ntp - 1), j)),
                           pl.BlockSpec((tl, cb), lambda i, j: (jnp.clip(i - ntp, 0, nts - 1), j))],
                 out_specs=blk, out_shape=jax.ShapeDtypeStruct((t, a_width), BF16), name="hy_post",
                 )(x0, z, bias.reshape(1, a_width), conv_p, conv_s)


def _qkv_prep_kernel(q_ref, k_ref, v_ref, qn_ref, kn_ref, cos_ref, sin_ref, qo_ref, ko_ref, vo_ref, kc_ref, *, hd, scale):
    cos = cos_ref[...]
    sin = sin_ref[...]
    lane = lax.broadcasted_iota(I32, cos.shape, 1)
    low = (lane % (hd // 2)) < (hd // 4)

    def norm(x, g):
        return x * lax.rsqrt(jnp.mean(x * x, axis=-1, keepdims=True) + EPS) * g

    def rope(x):
        partner = jnp.where(low, pltpu.roll(x, hd - hd // 4, 1), pltpu.roll(x, hd // 4, 1))
        return x * cos + partner * sin

    for h in range(q_ref.shape[1] // hd):
        sl = slice(h * hd, (h + 1) * hd)
        qo_ref[:, sl] = (rope(norm(q_ref[:, sl], qn_ref[...])) * scale).astype(qo_ref.dtype)
    for h in range(k_ref.shape[1] // hd):
        sl = slice(h * hd, (h + 1) * hd)
        kn = norm(k_ref[:, sl], kn_ref[...])
        kc_ref[:, sl] = kn
        ko_ref[:, sl] = rope(kn).astype(ko_ref.dtype)
    vo_ref[...] = v_ref[...].astype(vo_ref.dtype)


def qkv_prep(u, col0, n_heads, n_kv, hd, qn, kn, cos_t, sin_t):
    t = u.shape[0]
    tl = TOK_TILE
    qw, kw = n_heads * hd, n_kv * hd
    assert col0 % qw == 0 and (col0 + qw) % kw == 0
    qb, kb = col0 // qw, (col0 + qw) // kw
    row = lambda w: pl.BlockSpec((tl, w), lambda i: (i, 0))
    vec = pl.BlockSpec((1, hd), lambda i: (0, 0))
    return _call(functools.partial(_qkv_prep_kernel, hd=hd, scale=hd ** -0.5), grid=(t // tl,),
                 in_specs=[pl.BlockSpec((tl, qw), lambda i: (i, qb)),
                           pl.BlockSpec((tl, kw), lambda i: (i, kb)),
                           pl.BlockSpec((tl, kw), lambda i: (i, kb + 1)),
                           vec, vec, row(hd), row(hd)],
                 out_specs=[row(qw), row(kw), row(kw), row(kw)],
                 out_shape=[jax.ShapeDtypeStruct((t, qw), BF16), jax.ShapeDtypeStruct((t, kw), BF16),
                            jax.ShapeDtypeStruct((t, kw), BF16), jax.ShapeDtypeStruct((t, kw), F32)],
                 name="qkv_prep")(u, u, u, qn.reshape(1, hd), kn.reshape(1, hd), cos_t, sin_t)


def _rope_tables(seq, hd):
    rows = seq // GRID_W
    row = jnp.repeat(jnp.arange(rows), GRID_W).astype(F32)
    col = jnp.tile(jnp.arange(GRID_W), rows).astype(F32)
    nf = hd // 4
    inv = jnp.exp(-math.log(ROPE_BASE) * jnp.arange(nf, dtype=F32) / nf)
    ang_r = row[:, None] * inv[None, :]
    ang_c = col[:, None] * inv[None, :]
    cos = jnp.concatenate([jnp.cos(ang_r)] * 2 + [jnp.cos(ang_c)] * 2, axis=-1)
    sin = jnp.concatenate([-jnp.sin(ang_r), jnp.sin(ang_r), -jnp.sin(ang_c), jnp.sin(ang_c)], axis=-1)
    return cos, sin


def _attn_kernel(sink_ref, q_ref, *refs, kinds, group, hd, nblk):
    nk = len(kinds)
    k_refs, v_refs, o_ref = refs[:nk], refs[nk:2 * nk], refs[2 * nk]
    kvg = pl.program_id(1)
    i = pl.program_id(2)
    tq = q_ref.shape[0]
    q = jnp.concatenate([q_ref[:, g * hd:(g + 1) * hd] for g in range(group)], axis=0)
    rows = lax.broadcasted_iota(I32, (group * tq, 1), 0)
    sink = jnp.full((group * tq, 1), sink_ref[kvg * group], F32)
    for g in range(1, group):
        sink = jnp.where(rows >= g * tq, sink_ref[kvg * group + g], sink)

    def load(ref):
        x = ref[0] if len(ref.shape) == 3 else ref[...]
        return x.astype(BF16)

    scores = []
    m = sink
    for kind, k_ref in zip(kinds, k_refs):
        s = lax.dot_general(q, load(k_ref), (((1,), (1,)), ((), ())), preferred_element_type=F32)
        if kind in ("prev", "next"):
            r = lax.broadcasted_iota(I32, s.shape, 0) % tq
            c = lax.broadcasted_iota(I32, s.shape, 1)
            if kind == "prev":
                ok = c >= r + jnp.where(i > 0, 0, 2 * tq)
            else:
                ok = c <= r - jnp.where(i < nblk - 1, 0, 2 * tq)
            s = jnp.where(ok, s, NEG_INF)
        scores.append(s)
        m = jnp.maximum(m, jnp.max(s, axis=-1, keepdims=True))
    den = jnp.exp(sink - m)
    acc = None
    for s, v_ref in zip(scores, v_refs):
        p = jnp.exp(s - m)
        den = den + jnp.sum(p, axis=-1, keepdims=True)
        pv = _bdot(p.astype(BF16), load(v_ref))
        acc = pv if acc is None else acc + pv
    out = acc / den
    o_ref[...] = jnp.concatenate([out[g * tq:(g + 1) * tq] for g in range(group)], axis=1).astype(o_ref.dtype)


def attn_context(q, k, v, sink, row0, nb, seq, n_kv, hd):
    group = q.shape[1] // (n_kv * hd)
    b0 = row0 // seq
    kv = pl.BlockSpec((seq, hd), lambda b, g, i, s: (b0 + b, g))
    return _call(functools.partial(_attn_kernel, kinds=("all",), group=group, hd=hd, nblk=1),
                 grid=(nb, n_kv, 1), nsp=1,
                 in_specs=[pl.BlockSpec((seq, group * hd), lambda b, g, i, s: (b0 + b, g)), kv, kv],
                 out_specs=pl.BlockSpec((seq, group * hd), lambda b, g, i, s: (b, g)),
                 out_shape=jax.ShapeDtypeStruct((nb * seq, q.shape[1]), BF16), name="attn_context",
                 )(sink, q, k, v)


def attn_latent(q, k, v, ck, cv, sink, row0, nb, seq, n_kv, hd):
    group = q.shape[1] // (n_kv * hd)
    blk = ATT_BLOCK
    nblk = seq // blk
    r0 = row0 // blk
    past = ck.shape[1]
    cur = lambda b, g, i, s: (r0 + b * nblk + i, g)
    prev = lambda b, g, i, s: (r0 + b * nblk + jnp.maximum(i - 1, 0), g)
    nxt = lambda b, g, i, s: (r0 + b * nblk + jnp.minimum(i + 1, nblk - 1), g)
    ctx = pl.BlockSpec((1, past, hd), lambda b, g, i, s: (b, 0, g))
    kvs = [pl.BlockSpec((blk, hd), f) for f in (prev, cur, nxt)]
    return _call(functools.partial(_attn_kernel, kinds=("prev", "cur", "next", "ctx"), group=group, hd=hd, nblk=nblk),
                 grid=(nb, n_kv, nblk), nsp=1,
                 in_specs=[pl.BlockSpec((blk, group * hd), cur)] + kvs + [ctx] + kvs + [ctx],
                 out_specs=pl.BlockSpec((blk, group * hd), lambda b, g, i, s: (b * nblk + i, g)),
                 out_shape=jax.ShapeDtypeStruct((nb * seq, q.shape[1]), BF16), name="attn_latent",
                 )(sink, q, k, k, k, ck, v, v, v, cv)


def _ret_kernel(lg_ref, q_ref, kt_ref, v_ref, g_ref, gn_ref, *refs, ch, nc, dk, has_init, emit_state):
    pos = 0
    if has_init:
        s0_refs = refs[0:2]
        pos = 2
    y_ref = refs[pos]
    pos += 1
    if emit_state:
        so_refs = refs[pos:pos + 2]
        pos += 2
    o_scr, s_scr = refs[pos], refs[pos + 1]
    h = pl.program_id(1)
    kscale = dk ** -0.5
    ii = lax.broadcasted_iota(I32, (ch, ch), 0).astype(F32)
    jj = lax.broadcasted_iota(I32, (ch, ch), 1).astype(F32)
    col = lax.broadcasted_iota(I32, (ch, 1), 0).astype(F32)
    lane = lax.broadcasted_iota(I32, (1, ch), 1).astype(F32)
    gn = gn_ref[0]

    consts = []
    for d in range(2):
        lg = lg_ref[d, h]
        diff = ii - jj if d == 0 else jj - ii
        dmat = jnp.where(diff >= 0, jnp.exp(lg * jnp.maximum(diff, 0.0)), 0.0) * kscale
        xi = jnp.exp(lg * (col + 1.0)) if d == 0 else jnp.exp(lg * (ch - col))
        zeta = (jnp.exp(lg * (ch - 1.0 - lane)) if d == 0 else jnp.exp(lg * lane)) * kscale
        cdec = jnp.exp(lg * jnp.full((1, 1), float(ch), F32))
        consts.append((dmat, xi, zeta, cdec))
        if has_init:
            s_scr[d] = s0_refs[d][0, 0, 0]
        else:
            s_scr[d] = jnp.zeros(s_scr.shape[1:], F32)

    def chunk_out(d, c):
        dmat, xi, zeta, cdec = consts[d]
        r0 = pl.multiple_of(c * ch, ch)
        qc = q_ref[0, pl.ds(r0, ch), :]
        ktc = kt_ref[:, pl.ds(r0, ch)]
        vc = v_ref[0, pl.ds(r0, ch), :]
        sc = _bdot(qc, ktc) * dmat
        s = s_scr[d]
        o = _bdot(sc.astype(BF16), vc) + _bdot(qc, s.astype(BF16)) * xi
        kz = (ktc.astype(F32) * zeta).astype(BF16)
        s_scr[d] = s * cdec + _bdot(kz, vc)
        return r0, o

    def finalize(r0, o):
        mu = jnp.mean(o, axis=-1, keepdims=True)
        var = jnp.mean(jnp.square(o - mu), axis=-1, keepdims=True)
        on = (o - mu) * lax.rsqrt(var + EPS) * gn
        gate = _silu(g_ref[0, pl.ds(r0, ch), :].astype(F32))
        y_ref[0, pl.ds(r0, ch), :] = (gate * on).astype(y_ref.dtype)

    def first_half(n, carry):
        for d in range(2):
            r0, o = chunk_out(d, n if d == 0 else nc - 1 - n)
            o_scr[pl.ds(r0, ch), :] = o
        return carry

    def second_half(n, carry):
        for d in range(2):
            r0, o = chunk_out(d, n if d == 0 else nc - 1 - n)
            finalize(r0, o_scr[pl.ds(r0, ch), :] + o)
        return carry

    lax.fori_loop(0, nc // 2, first_half, 0)
    lax.fori_loop(nc // 2, nc, second_half, 0)
    if emit_state:
        for d in range(2):
            so_refs[d][0, 0, 0] = s_scr[d]


def retention(q, kt, vg, row0, nb, seq, n_heads, dk, dv, log_gamma, gn, *, s_f0=None, s_b0=None, emit_state=False):
    t = q.shape[0]
    q3 = q.reshape(t // seq, seq, n_heads * dk)
    vg3 = vg.reshape(t // seq, seq, 2 * n_heads * dv)
    b0 = row0 // seq
    has_init = s_f0 is not None
    specs = [pl.BlockSpec(memory_space=pltpu.SMEM),
             pl.BlockSpec((1, seq, dk), lambda b, h: (b0 + b, 0, h)),
             pl.BlockSpec((dk, seq), lambda b, h: (h, b0 + b)),
             pl.BlockSpec((1, seq, dv), lambda b, h: (b0 + b, 0, h)),
             pl.BlockSpec((1, seq, dv), lambda b, h: (b0 + b, 0, n_heads + h)),
             pl.BlockSpec((1, 1, dv), lambda b, h: (h, 0, 0))]
    args = [log_gamma, q3, kt, vg3, vg3, gn.reshape(n_heads, 1, dv)]
    st = pl.BlockSpec((1, 1, 1, dk, dv), lambda b, h: (b, 0, h, 0, 0))
    if has_init:
        specs += [st, st]
        args += [s_f0, s_b0]
    out_specs = [pl.BlockSpec((1, seq, dv), lambda b, h: (b, 0, h))]
    out_shape = [jax.ShapeDtypeStruct((nb, seq, n_heads * dv), BF16)]
    if emit_state:
        out_specs += [st, st]
        out_shape += [jax.ShapeDtypeStruct((nb, 1, n_heads, dk, dv), F32)] * 2
    gs = pl.GridSpec(grid=(nb, n_heads), in_specs=specs, out_specs=out_specs,
                     scratch_shapes=[pltpu.VMEM((seq, dv), F32), pltpu.VMEM((2, dk, dv), F32)])
    ch = 2 * RET_CHUNK if seq % (4 * RET_CHUNK) == 0 else RET_CHUNK
    outs = pl.pallas_call(
        functools.partial(_ret_kernel, ch=ch, nc=seq // ch, dk=dk, has_init=has_init, emit_state=emit_state),
        grid_spec=gs, out_shape=out_shape, name="retention",
        compiler_params=pltpu.CompilerParams(dimension_semantics=("arbitrary", "arbitrary"),
                                             vmem_limit_bytes=VMEM_LIMIT))(*args)
    y = outs[0].reshape(nb * seq, n_heads * dv)
    return (y,) + tuple(outs[1:])


def _ffn_up_kernel(idx_ref, h_hbm, w1_ref, w3_ref, o_ref, xf_scr, xb_scr, sem, *, tm, nf):
    tile = pl.program_id(0)
    f = pl.program_id(1)
    slot = tile % 2
    per = tm // nf

    def row_copy(row, dst_slot, r):
        return pltpu.make_async_copy(h_hbm.at[pl.ds(row, 1)], xf_scr.at[dst_slot, pl.ds(r, 1)], sem.at[dst_slot])

    def wait_rows(src_slot):
        pltpu.make_async_copy(h_hbm.at[pl.ds(0, tm)], xf_scr.at[src_slot], sem.at[src_slot]).wait()

    @pl.when(jnp.logical_and(tile == 0, f == 0))
    def _():
        def issue(r, carry):
            row_copy(idx_ref[r], 0, r).start()
            return carry

        lax.fori_loop(0, tm, issue, 0)

    @pl.when(f == 0)
    def _():
        wait_rows(slot)
        xb_scr[...] = xf_scr[slot].astype(BF16)

    base = (tile + 1) * tm + f * per
    for r in range(per):
        row_copy(idx_ref[base + r], 1 - slot, f * per + r).start()

    x = xb_scr[...]
    a = _bdot(x, w1_ref[0, 0].astype(BF16))
    b = _bdot(x, w3_ref[0, 0].astype(BF16))
    o_ref[...] = (_silu(a) * b).astype(o_ref.dtype)

    @pl.when(jnp.logical_and(tile == pl.num_programs(0) - 1, f == nf - 1))
    def _():
        wait_rows(1 - slot)


def ffn_up(h, rows, w1, w3, layer, tm, tiles_per_expert):
    _, n_exp, d, ff = w1.shape
    nslots = rows.shape[0]
    tf = _pick(ff, (512, 256, 128))
    nf = ff // tf
    tpe = tiles_per_expert
    rows = jnp.pad(rows, (0, tm))
    wspec = pl.BlockSpec((1, 1, d, tf), lambda t, f, idx: (layer, t // tpe, 0, f))
    return _call(functools.partial(_ffn_up_kernel, tm=tm, nf=nf), grid=(nslots // tm, nf), nsp=1,
                 in_specs=[pl.BlockSpec(memory_space=pl.ANY), wspec, wspec],
                 out_specs=pl.BlockSpec((tm, tf), lambda t, f, idx: (t, f)),
                 out_shape=jax.ShapeDtypeStruct((nslots, ff), BF16),
                 scratch=[pltpu.VMEM((2, tm, d), F32), pltpu.VMEM((tm, d), BF16), pltpu.SemaphoreType.DMA((2,))],
                 name="ffn_up")(rows, h, w1, w3)


def _ffn_down_kernel(x_ref, w_ref, g_ref, o_ref, wb):
    @pl.when(pl.program_id(2) == 0)
    def _():
        wb[...] = w_ref[0, 0].astype(BF16)

    acc = _bdot(x_ref[...], wb[...])
    g = g_ref[...]
    for c in range(acc.shape[1] // LANE):
        o_ref[:, c * LANE:(c + 1) * LANE] = acc[:, c * LANE:(c + 1) * LANE] * g


def ffn_down(hm, w2, layer, gates, tm, tiles_per_expert):
    _, n_exp, ff, d = w2.shape
    nslots = hm.shape[0]
    tn = _pick(d, (512, 256, 128))
    tpe = tiles_per_expert
    return _call(_ffn_down_kernel, grid=(n_exp, d // tn, tpe),
                 in_specs=[pl.BlockSpec((tm, ff), lambda e, j, r: (e * tpe + r, 0)),
                           pl.BlockSpec((1, 1, ff, tn), lambda e, j, r: (layer, e, 0, j)),
                           pl.BlockSpec((tm, LANE), lambda e, j, r: (e * tpe + r, 0))],
                 out_specs=pl.BlockSpec((tm, tn), lambda e, j, r: (e * tpe + r, j)),
                 out_shape=jax.ShapeDtypeStruct((nslots, d), F32),
                 scratch=[pltpu.VMEM((ff, tn), BF16)], name="ffn_down")(hm, w2, gates)


SLOT_GROUP = 8
COMBINE_BATCH = 32
COMBINE_TILE = 512


def _combine_kernel(flat_ref, pstart_ref, cidx_ref, y_hbm, t_hbm, x_ref, g_ref, *refs, nsplit):
    nout = 1 if nsplit is None else 2
    o_refs = refs[:nout]
    z_scr, tz_scr, acc_scr, st_ref, sem = refs[nout:]
    i = pl.program_id(0)
    tl = x_ref.shape[0]
    gb = COMBINE_BATCH
    rows_b = gb * SLOT_GROUP
    p0, p1 = pstart_ref[i], pstart_ref[i + 1]
    npos = p1 - p0
    nbatch = (npos + gb - 1) // gb
    next_has = jnp.logical_and(i + 1 < pl.num_programs(0), pstart_ref[i + 2] > p1)

    @pl.when(i == 0)
    def _():
        st_ref[0] = 0
        st_ref[1] = 0

    s0 = st_ref[0]
    started = st_ref[1]

    def gather(pbase, slot):
        def issue(k, carry):
            r = pl.multiple_of(flat_ref[pbase + k] * SLOT_GROUP, SLOT_GROUP)
            ko = pl.multiple_of(k * SLOT_GROUP, SLOT_GROUP)
            pltpu.make_async_copy(y_hbm.at[pl.ds(r, SLOT_GROUP)], z_scr.at[slot, pl.ds(ko, SLOT_GROUP)],
                                  sem.at[slot]).start()
            pltpu.make_async_copy(t_hbm.at[pl.ds(r, SLOT_GROUP)], tz_scr.at[slot, pl.ds(ko, SLOT_GROUP)],
                                  sem.at[slot]).start()
            return carry

        lax.fori_loop(0, gb, issue, 0, unroll=8)

    @pl.when(jnp.logical_and(nbatch > 0, started == 0))
    def _():
        gather(p0, s0)

    acc_scr[...] = jnp.zeros_like(acc_scr)
    t_col = (i * tl + lax.broadcasted_iota(I32, (tl, 1), 0)).astype(F32)
    lane_j = lax.broadcasted_iota(I32, (1, rows_b), 1)

    def batch(bi, carry):
        slot = (s0 + bi) % 2

        @pl.when(bi + 1 < nbatch)
        def _():
            gather(p0 + (bi + 1) * gb, 1 - slot)

        @pl.when(jnp.logical_and(bi + 1 == nbatch, next_has))
        def _():
            gather(p1, 1 - slot)

        pltpu.make_async_copy(y_hbm.at[pl.ds(0, rows_b)], z_scr.at[slot], sem.at[slot]).wait()
        pltpu.make_async_copy(t_hbm.at[pl.ds(0, rows_b)], tz_scr.at[slot], sem.at[slot]).wait()
        tok_row = tz_scr[slot].T[0:1, :]
        tok_row = jnp.where(lane_j < (npos - bi * gb) * SLOT_GROUP, tok_row, -1.0)
        onehot = jnp.where(tok_row == t_col, 1.0, 0.0).astype(BF16)
        acc_scr[...] += _bdot(onehot, z_scr[slot].astype(BF16))
        return carry

    lax.fori_loop(0, nbatch, batch, 0)

    @pl.when(nbatch > 0)
    def _():
        st_ref[0] = (s0 + nbatch) % 2
        st_ref[1] = next_has.astype(I32)

    @pl.when(nbatch == 0)
    def _():
        st_ref[1] = 0

    out = x_ref[...] + g_ref[0, 0] * acc_scr[...]
    if nsplit is None:
        o_refs[0][...] = out
    else:
        @pl.when(i < nsplit)
        def _():
            o_refs[0][...] = out

        @pl.when(i >= nsplit)
        def _():
            o_refs[1][...] = out


def moe_combine(x, y, toks, flat, pstart, mods, cidx, k_gate, split=None):
    t, d = x.shape
    tl = COMBINE_TILE
    stride = tl // TOK_TILE
    blk = pl.BlockSpec((tl, d), lambda i, *_: (i, 0))
    if split is None:
        nsplit, out_specs, out_shape = None, blk, jax.ShapeDtypeStruct((t, d), F32)
    else:
        nsplit, n1 = split // tl, (t - split) // tl
        out_specs = [pl.BlockSpec((tl, d), lambda i, *_: (jnp.minimum(i, nsplit - 1), 0)),
                     pl.BlockSpec((tl, d), lambda i, *_: (jnp.clip(i - nsplit, 0, n1 - 1), 0))]
        out_shape = [jax.ShapeDtypeStruct((split, d), F32), jax.ShapeDtypeStruct((t - split, d), F32)]
    rows_b = COMBINE_BATCH * SLOT_GROUP
    return _call(functools.partial(_combine_kernel, nsplit=nsplit), grid=(t // tl,), nsp=3,
                 in_specs=[pl.BlockSpec(memory_space=pl.ANY), pl.BlockSpec(memory_space=pl.ANY), blk,
                           pl.BlockSpec((1, 1, 1, d), lambda i, f, p, c: (c[i * stride], k_gate, 0, 0))],
                 out_specs=out_specs, out_shape=out_shape,
                 scratch=[pltpu.VMEM((2, rows_b, d), F32), pltpu.VMEM((2, rows_b, LANE), F32),
                          pltpu.VMEM((tl, d), F32), pltpu.SMEM((2,), I32), pltpu.SemaphoreType.DMA((2,))],
                 name="moe_combine")(flat, pstart, cidx, y, toks, x, mods)


def _route(aff_t, groups):
    n_exp = aff_t.shape[0]
    caps = [CAPACITY_FACTOR * seq // n_exp for _, _, seq in groups]
    slots = sum(nb * cap for (_, nb, _), cap in zip(groups, caps))
    e_ids = jnp.arange(n_exp, dtype=I32)[:, None, None]
    rows, gates, g0s, ngs = [], [], [], []
    off = 0
    for (row0, nb, seq), cap in zip(groups, caps):
        a = lax.slice_in_dim(aff_t, row0, row0 + nb * seq, axis=1).reshape(n_exp, nb, seq)
        gate, idx = lax.top_k(a, cap)
        idx, gate = lax.sort((idx.astype(I32), gate), dimension=2, num_keys=1)
        b_ids = jnp.arange(nb, dtype=I32)[None, :, None]
        gtok = (b_ids * seq + idx).reshape(n_exp, nb * cap)
        rows.append(row0 + gtok)
        gates.append(gate.reshape(n_exp, nb * cap))
        ntile = nb * seq // COMBINE_TILE
        bounds = jnp.arange(ntile + 1, dtype=I32) * COMBINE_TILE
        cnt = jnp.sum(gtok[:, None, :] < bounds[None, :, None], axis=-1, dtype=I32)
        base = e_ids[:, :, 0] * slots + off
        lo, hi = base + cnt[:, :-1], base + cnt[:, 1:]
        g0 = lo // SLOT_GROUP
        ng = jnp.where(hi > lo, (hi + SLOT_GROUP - 1) // SLOT_GROUP - g0, 0)
        g0s.append(g0.T)
        ngs.append(ng.T)
        off += nb * cap
    g0 = jnp.concatenate(g0s, axis=0).reshape(-1)
    ng = jnp.concatenate(ngs, axis=0).reshape(-1)
    ntiles = g0.shape[0] // n_exp
    cum = jnp.cumsum(ng)
    start = cum - ng
    pstart = jnp.concatenate([start[::n_exp], cum[-1:], cum[-1:]]).astype(I32)
    pmax = n_exp * slots // SLOT_GROUP + 2 * ntiles * n_exp + COMBINE_BATCH
    p = jnp.arange(pmax, dtype=I32)
    delta = g0 - start
    step = delta - jnp.concatenate([jnp.zeros((1,), I32), delta[:-1]])
    offs = jnp.sum(jnp.where(p[:, None] >= start[None, :], step[None, :], 0), axis=1, dtype=I32)
    flat = jnp.where(p < cum[-1], p + offs, 0).astype(I32)
    return jnp.concatenate(rows, axis=1), jnp.concatenate(gates, axis=1), flat, pstart


def ec_moe(x, gain, mods, cidx, router, w1, w3, w2, layer, groups, split=None):
    n_exp = router.shape[1]
    h, aff_t = norm_mod(x, gain, mods, cidx, 3, 4, router_t=router.T)
    rows, gates, flat, pstart = _route(aff_t, groups)
    slots = rows.shape[1]
    tm = _pick(slots, (1024, 512, 256, 128, 64, 32, 16, 8))
    tpe = slots // tm
    rows_f = rows.reshape(-1)
    hm = ffn_up(h, rows_f, w1, w3, layer, tm, tpe)
    lanes = lambda v: jnp.broadcast_to(v.reshape(-1, 1), (n_exp * slots, LANE))
    tm_down = _pick(slots, (2560, 1280, 1024, 512, 256, 128, 64, 32, 16, 8))
    y = ffn_down(hm, w2, layer, lanes(gates), tm_down, slots // tm_down)
    return moe_combine(x, y, lanes(rows_f.astype(F32)), flat, pstart, mods, cidx, 5, split)


def kernel(x_prompt, x_sample, cache_attn_k, cache_attn_v, state_ret_fwd, state_ret_bwd, c, c_ctx, ada_w, ada_b, norm1_g, norm2_g, ev_w_in, ev_w_out, hy_sconv_w, hy_sconv_b, hy_ffn_w1, hy_ffn_b1, hy_ffn_w2, hy_ffn_b2, hy_ffn_w3, hy_freq, hy_bias, at_q_norm, at_k_norm, at_sink, od_w_in, od_w_out, ret_decay, ret_gn, moe_router, moe_w1, moe_w3, moe_w2):
    bp, lp, d = x_prompt.shape
    bs, ls, _ = x_sample.shape
    depth = ada_w.shape[0]
    tp, ts = bp * lp, bs * ls
    t = tp + ts
    groups = ((0, bp, lp), (tp, bs, ls))
    a_width = hy_bias.shape[-1]
    hd = at_q_norm.shape[-1]
    n_heads = at_sink.shape[-1]
    n_kv = cache_attn_k.shape[3]
    c_heads, c_dv = ret_gn.shape[1], ret_gn.shape[2]
    c_dk = (od_w_in.shape[-1] - 2 * c_heads * c_dv) // (2 * c_heads)

    ntp, nts = tp // TOK_TILE, ts // TOK_TILE
    tps, tss = lp // TOK_TILE, ls // TOK_TILE
    tile = np.arange(ntp + nts)
    cidx = jnp.asarray(np.where(tile < ntp, 0, 1 + (tile - ntp) // tss), I32)
    in_seq = np.where(tile < ntp, tile % tps, (tile - ntp) % tss)
    seq_tiles = np.where(tile < ntp, tps, tss)
    first = jnp.asarray(in_seq == 0, I32)
    last = jnp.asarray(in_seq == seq_tiles - 1, I32)
    tm_mm = _pick(math.gcd(tp, ls), (1024, 512, 256))
    tm_big = _pick(t, (2048, 1024, 512, 256))

    cond = jnp.concatenate([c_ctx[None, :], c], axis=0)
    cond = jnp.pad(cond, ((0, (-cond.shape[0]) % 8), (0, 0)))
    mods_all = adaln_all(cond, ada_w, ada_b)
    x = (x_prompt.reshape(tp, d), x_sample.reshape(ts, d))

    new_k, new_v, new_sf, new_sb = [], [], [], []
    for l in range(depth):
        mods = mods_all[l].reshape(-1, 6, 1, d)
        j = l // 2
        h = norm_mod(x, norm1_g[l], mods, cidx, 0, 1)
        if l % 2 == 0:
            u = matmul([h], ev_w_in[j], out_dtype=F32, tm=tm_big)
            z, x0 = hy_pre(u, hy_sconv_w[j], hy_sconv_b[j], first, last, a_width)
            convs = []
            for row0, nb, seq in groups:
                taps = hyena_taps(seq, hy_ffn_w1[j], hy_ffn_b1[j], hy_ffn_w2[j], hy_ffn_b2[j],
                                  hy_ffn_w3[j], hy_freq[j], a_width)
                conv_fn = long_conv_direct if seq <= 512 else long_conv_two_stage
                convs.append(conv_fn(z, row0, nb, seq, taps))
            ya = hy_post(x0, z, hy_bias[j], convs[0], convs[1])
            cos_s, sin_s = _rope_tables(ls, hd)
            cos_t = jnp.concatenate([jnp.ones((tp, hd), F32), jnp.tile(cos_s, (bs, 1))], axis=0)
            sin_t = jnp.concatenate([jnp.zeros((tp, hd), F32), jnp.tile(sin_s, (bs, 1))], axis=0)
            q, k, v, k_normed = qkv_prep(u, 3 * a_width, n_heads, n_kv, hd, at_q_norm[j], at_k_norm[j], cos_t, sin_t)
            yb_p = attn_context(q, k, v, at_sink[j], 0, bp, lp, n_kv, hd)
            past = cache_attn_k.shape[2]
            ck = cache_attn_k[:, j].reshape(bs, past, n_kv * hd)
            cv = cache_attn_v[:, j].reshape(bs, past, n_kv * hd)
            yb_s = attn_latent(q, k, v, ck, cv, at_sink[j], tp, bs, ls, n_kv, hd)
            x = matmul([ya, (yb_p, yb_s)], ev_w_out[j], out_dtype=F32, tm=tm_mm, res=x, mods=mods, cidx=cidx, k_gate=2)
            kv0 = 3 * a_width + n_heads * hd
            new_k.append(k_normed[:tp].reshape(bp, 1, lp, n_kv, hd))
            new_v.append(u[:tp, kv0 + n_kv * hd:kv0 + 2 * n_kv * hd].reshape(bp, 1, lp, n_kv, hd))
        else:
            nqk = c_heads * c_dk
            w_in = od_w_in[j]
            q = matmul([h], w_in, out_dtype=BF16, tm=tm_big, ncols=nqk)
            kt = matmul_t(h, w_in[:, nqk:2 * nqk].T, out_dtype=BF16, tm=tm_big)
            vg = matmul([h], w_in, out_dtype=BF16, tm=tm_big, col0=2 * nqk)
            log_gamma = -jnp.exp(ret_decay[j].astype(F32))
            y_p, sf, sb = retention(q, kt, vg, 0, bp, lp, c_heads, c_dk, c_dv, log_gamma, ret_gn[j], emit_state=True)
            (y_s,) = retention(q, kt, vg, tp, bs, ls, c_heads, c_dk, c_dv, log_gamma, ret_gn[j],
                               s_f0=state_ret_fwd[:, j:j + 1], s_b0=state_ret_bwd[:, j:j + 1])
            x = matmul([(y_p, y_s)], od_w_out[j], out_dtype=F32, tm=min(tm_mm, 512), res=x, mods=mods, cidx=cidx,
                       k_gate=2)
            new_sf.append(sf)
            new_sb.append(sb)
        x = ec_moe(x, norm2_g[l], mods, cidx, moe_router[l], moe_w1, moe_w3, moe_w2, l, groups,
                   split=tp if l == depth - 1 else None)

    cat = lambda parts: jnp.concatenate(parts, axis=1)
    return (x[0].reshape(bp, lp, d), x[1].reshape(bs, ls, d), cat(new_k), cat(new_v), cat(new_sf), cat(new_sb))
```

```python
import functools
import math

import ml_dtypes
import numpy as np
import jax
import jax.numpy as jnp
from jax import lax
from jax.experimental import pallas as pl
from jax.experimental.pallas import tpu as pltpu

F32 = jnp.float32
BF16 = jnp.bfloat16
I32 = jnp.int32
HIGHEST = lax.Precision.HIGHEST

EPS = 1e-6
NEG_INF = -1e30
GRID_W = 64
WINDOW = 128
ATT_BLOCK = 128
RET_CHUNK = 128
ROPE_BASE = 10000.0
A_BANDS = 16
A_SHORT_DECAY_PCT = 0.3
A_LONG_DECAY_PCT = 1.5
A_DECAY_TARGET = 1e-2
CAPACITY_FACTOR = 2

TOK_TILE = 256
LANE = 128
FFT_N2 = 128
VMEM_LIMIT = 56 * 1024 * 1024


def _pick(n, cands):
    for c in cands:
        if n % c == 0:
            return c
    raise ValueError(f"no tile for {n} in {cands}")


def _call(kernel, *, grid, in_specs, out_specs, out_shape, scratch=(), nsp=0, name=None):
    gs = pltpu.PrefetchScalarGridSpec(num_scalar_prefetch=nsp, grid=grid, in_specs=in_specs,
                                      out_specs=out_specs, scratch_shapes=list(scratch))
    cp = pltpu.CompilerParams(dimension_semantics=("arbitrary",) * len(grid), vmem_limit_bytes=VMEM_LIMIT)
    return pl.pallas_call(kernel, grid_spec=gs, out_shape=out_shape, compiler_params=cp, name=name)


def _silu(x):
    return x / (1.0 + jnp.exp(-x))


def _bdot(a, b):
    return jnp.dot(a, b, preferred_element_type=F32)


def _split(x):
    hi = x.astype(BF16)
    lo = (x - hi.astype(F32)).astype(BF16)
    return hi, lo


def _dot3(mh, ml, x):
    xh, xl = _split(x)
    return _bdot(mh, xh) + _bdot(mh, xl) + _bdot(ml, xh)


def _adaln_kernel(c_ref, w_ref, b_ref, o_ref):
    s = _silu(c_ref[...])
    o_ref[0] = _bdot(s.astype(BF16), w_ref[0].astype(BF16)) + b_ref[0]


def adaln_all(cond, ada_w, ada_b):
    depth, d, n6 = ada_w.shape
    rows = cond.shape[0]
    tn = _pick(n6, (1024, 512, 256, 128))
    return _call(
        _adaln_kernel, grid=(depth, n6 // tn),
        in_specs=[pl.BlockSpec((rows, d), lambda l, j: (0, 0)),
                  pl.BlockSpec((1, d, tn), lambda l, j: (l, 0, j)),
                  pl.BlockSpec((1, 1, tn), lambda l, j: (l, 0, j))],
        out_specs=pl.BlockSpec((1, rows, tn), lambda l, j: (l, 0, j)),
        out_shape=jax.ShapeDtypeStruct((depth, rows, n6), F32), name="adaln",
    )(cond, ada_w, ada_b.reshape(depth, 1, n6))


def _row_sources(a, tm):
    if not isinstance(a, tuple):
        return [a], [lambda i: i], None
    a0, a1 = a
    n0, n1 = a0.shape[0] // tm, a1.shape[0] // tm
    return [a0, a1], [lambda i: jnp.minimum(i, n0 - 1), lambda i: jnp.clip(i - n0, 0, n1 - 1)], n0


def _load_rows(refs, i, split):
    if split is None:
        return refs[0][...]
    return jnp.where(i < split, refs[0][...], refs[1][...])


def _norm_mod(x, g_ref, sc_ref, sh_ref):
    ms = jnp.mean(x * x, axis=-1, keepdims=True)
    y = x * lax.rsqrt(ms + EPS) * g_ref[...]
    return y * (1.0 + sc_ref[0, 0]) + sh_ref[0, 0]


def _norm_mod_kernel(cidx_ref, *refs, nsrc, split):
    g_ref, sc_ref, sh_ref, o_ref = refs[nsrc:]
    x = _load_rows(refs[:nsrc], pl.program_id(0), split)
    o_ref[...] = _norm_mod(x, g_ref, sc_ref, sh_ref).astype(o_ref.dtype)


def _norm_router_kernel(cidx_ref, *refs, nsrc, split):
    g_ref, sc_ref, sh_ref, rt_ref, h_ref, aff_ref = refs[nsrc:]
    x = _load_rows(refs[:nsrc], pl.program_id(0), split)
    h = _norm_mod(x, g_ref, sc_ref, sh_ref)
    h_ref[...] = h
    nt = lambda a, b: lax.dot_general(a, b, (((1,), (1,)), ((), ())), preferred_element_type=F32)
    rh, rl = _split(rt_ref[...])
    hh, hl = _split(h)
    logits = nt(rh, hh) + nt(rh, hl) + nt(rl, hh)
    m = jnp.max(logits, axis=0, keepdims=True)
    e = jnp.exp(logits - m)
    aff_ref[...] = e / jnp.sum(e, axis=0, keepdims=True)


def norm_mod(x, gain, mods, cidx, k_shift, k_scale, *, router_t=None):
    tm = TOK_TILE
    srcs, fns, split = _row_sources(x, tm)
    t = sum(a.shape[0] for a in srcs)
    d = srcs[0].shape[1]
    specs = [pl.BlockSpec((tm, d), lambda i, c, f=f: (f(i), 0)) for f in fns]
    specs += [pl.BlockSpec((1, d), lambda i, c: (0, 0)),
              pl.BlockSpec((1, 1, 1, d), lambda i, c: (c[i], k_scale, 0, 0)),
              pl.BlockSpec((1, 1, 1, d), lambda i, c: (c[i], k_shift, 0, 0))]
    kw = dict(nsrc=len(srcs), split=split)
    if router_t is None:
        return _call(functools.partial(_norm_mod_kernel, **kw), grid=(t // tm,), nsp=1, in_specs=specs,
                     out_specs=pl.BlockSpec((tm, d), lambda i, c: (i, 0)),
                     out_shape=jax.ShapeDtypeStruct((t, d), BF16), name="norm_mod",
                     )(cidx, *srcs, gain.reshape(1, d), mods, mods)
    e = router_t.shape[0]
    specs.append(pl.BlockSpec((e, d), lambda i, c: (0, 0)))
    return _call(functools.partial(_norm_router_kernel, **kw), grid=(t // tm,), nsp=1, in_specs=specs,
                 out_specs=[pl.BlockSpec((tm, d), lambda i, c: (i, 0)),
                            pl.BlockSpec((e, tm), lambda i, c: (0, i))],
                 out_shape=[jax.ShapeDtypeStruct((t, d), F32), jax.ShapeDtypeStruct((e, t), F32)],
                 name="norm_router",
                 )(cidx, *srcs, gain.reshape(1, d), mods, mods, router_t)


def _mm_kernel(*refs, xcounts, xsplits, gated, rcount, rsplit):
    pos = 1 if gated else 0
    i = pl.program_id(1)
    x_refs = []
    for cnt in xcounts:
        x_refs.append(refs[pos:pos + cnt])
        pos += cnt
    nx = len(xcounts)
    w_refs = refs[pos:pos + nx]
    pos += nx
    if gated:
        res_refs, gate_ref = refs[pos:pos + rcount], refs[pos + rcount]
        pos += rcount + 1
    o_ref = refs[pos]
    wb_refs = refs[pos + 1:]

    @pl.when(i == 0)
    def _():
        for w_ref, wb in zip(w_refs, wb_refs):
            wb[...] = w_ref[...].astype(BF16)

    acc = None
    for xr, split, wb in zip(x_refs, xsplits, wb_refs):
        part = _bdot(_load_rows(xr, i, split), wb[...])
        acc = part if acc is None else acc + part
    if gated:
        acc = _load_rows(res_refs, i, rsplit) + gate_ref[0, 0] * acc
    o_ref[...] = acc.astype(o_ref.dtype)


def matmul(xs, w, *, out_dtype, tm, tn=512, col0=0, ncols=None, res=None, mods=None, cidx=None, k_gate=None):
    srcs = [_row_sources(x, tm) for x in xs]
    m = sum(a.shape[0] for a in srcs[0][0])
    kk = srcs[0][0][0].shape[1]
    nx = len(xs)
    assert w.shape[0] == nx * kk
    n = w.shape[1] - col0 if ncols is None else ncols
    tn = _pick(math.gcd(n, col0) if col0 else n, (tn, 512, 256, 128))
    cb0 = col0 // tn
    gated = res is not None
    if gated:
        imap = lambda f: (lambda j, i, c: f(j, i, c))
    else:
        imap = lambda f: (lambda j, i: f(j, i, None))
    specs, args = [], []
    for arrs, fns, _ in srcs:
        specs += [pl.BlockSpec((tm, kk), imap(lambda j, i, c, f=f: (f(i), 0))) for f in fns]
        args += arrs
    specs += [pl.BlockSpec((kk, tn), imap(lambda j, i, c, q=q: (q, cb0 + j))) for q in range(nx)]
    args += [w] * nx
    rcount, rsplit = 0, None
    if gated:
        assert mods.shape[-1] == n and col0 == 0
        rarrs, rfns, rsplit = _row_sources(res, tm)
        rcount = len(rarrs)
        specs += [pl.BlockSpec((tm, tn), imap(lambda j, i, c, f=f: (f(i), j))) for f in rfns]
        stride = tm // TOK_TILE
        specs.append(pl.BlockSpec((1, 1, 1, tn), imap(lambda j, i, c: (c[i * stride], k_gate, 0, j))))
        args = [cidx] + args + rarrs + [mods]
    kern = functools.partial(_mm_kernel, xcounts=tuple(len(s[0]) for s in srcs),
                             xsplits=tuple(s[2] for s in srcs), gated=gated, rcount=rcount, rsplit=rsplit)
    return _call(kern, grid=(n // tn, m // tm), nsp=1 if gated else 0, in_specs=specs,
                 out_specs=pl.BlockSpec((tm, tn), imap(lambda j, i, c: (i, j))),
                 out_shape=jax.ShapeDtypeStruct((m, n), out_dtype),
                 scratch=[pltpu.VMEM((kk, tn), BF16) for _ in xs], name="matmul",
                 )(*args)


def _mm_t_kernel(x_ref, w_ref, o_ref, wb):
    @pl.when(pl.program_id(1) == 0)
    def _():
        wb[...] = w_ref[...].astype(BF16)

    o_ref[...] = lax.dot_general(wb[...], x_ref[...], (((1,), (1,)), ((), ())),
                                 preferred_element_type=F32).astype(o_ref.dtype)


def matmul_t(x, wt, *, out_dtype, tm, tn=512):
    m, kk = x.shape
    n = wt.shape[0]
    tn = _pick(n, (tn, 512, 256, 128))
    return _call(_mm_t_kernel, grid=(n // tn, m // tm),
                 in_specs=[pl.BlockSpec((tm, kk), lambda j, i: (i, 0)), pl.BlockSpec((tn, kk), lambda j, i: (j, 0))],
                 out_specs=pl.BlockSpec((tn, tm), lambda j, i: (j, i)),
                 out_shape=jax.ShapeDtypeStruct((n, m), out_dtype),
                 scratch=[pltpu.VMEM((tn, kk), BF16)], name="matmul_t")(x, wt)


def _hy_pre_kernel(first_ref, last_ref, *refs):
    ins = refs[:15]
    z_ref, x0_ref = refs[15], refs[16]
    i = pl.program_id(0)
    is_first = first_ref[i] == 1
    is_last = last_ref[i] == 1

    def sconv(m_ref, p_ref, n_ref, w_ref, b_ref):
        x = m_ref[...]
        tl = x.shape[0]
        rows = lax.broadcasted_iota(I32, x.shape, 0)
        prev_row = jnp.where(is_first, 0.0, p_ref[7:8, :])
        next_row = jnp.where(is_last, 0.0, n_ref[0:1, :])
        xm1 = jnp.where(rows == 0, prev_row, pltpu.roll(x, 1, 0))
        xp1 = jnp.where(rows == tl - 1, next_row, pltpu.roll(x, tl - 1, 0))
        w = w_ref[...]
        return xm1 * w[0:1] + x * w[1:2] + xp1 * w[2:3] + b_ref[...]

    hv = sconv(ins[0], ins[1], ins[2], ins[9], ins[12])
    x0 = sconv(ins[3], ins[4], ins[5], ins[10], ins[13])
    x1 = sconv(ins[6], ins[7], ins[8], ins[11], ins[14])
    z_ref[...] = x1 * hv
    x0_ref[...] = x0


def hy_pre(u, sc_w, sc_b, first, last, a_width):
    t = u.shape[0]
    tl = TOK_TILE
    cb = _pick(a_width, (512, 256, 128))
    ncb = a_width // cb
    nt8 = t // 8
    specs = []
    for part in range(3):
        off = part * ncb
        specs += [pl.BlockSpec((tl, cb), lambda i, j, f, l, off=off: (i, off + j)),
                  pl.BlockSpec((8, cb), lambda i, j, f, l, off=off: (jnp.maximum(i * (tl // 8) - 1, 0), off + j)),
                  pl.BlockSpec((8, cb), lambda i, j, f, l, off=off: (jnp.minimum((i + 1) * (tl // 8), nt8 - 1), off + j))]
    for part in range(3):
        off = part * ncb
        specs.append(pl.BlockSpec((3, cb), lambda i, j, f, l, off=off: (0, off + j)))
    for part in range(3):
        off = part * ncb
        specs.append(pl.BlockSpec((1, cb), lambda i, j, f, l, off=off: (0, off + j)))
    out_spec = pl.BlockSpec((tl, cb), lambda i, j, f, l: (i, j))
    scb = sc_b.reshape(1, -1)
    return _call(_hy_pre_kernel, grid=(t // tl, ncb), nsp=2, in_specs=specs,
                 out_specs=[out_spec, out_spec],
                 out_shape=[jax.ShapeDtypeStruct((t, a_width), F32)] * 2, name="hy_pre",
                 )(first, last, *([u] * 9), sc_w, sc_w, sc_w, scb, scb, scb)


def _filter_kernel(fa_ref, fb_ref, w1_ref, b1_ref, w2_ref, b2_ref, fr_ref, w3f_ref, w3b_ref, dl_ref, o_ref):
    hdot = functools.partial(jnp.dot, precision=HIGHEST, preferred_element_type=F32)

    def taps(feats, w3_ref):
        h = jnp.sin(fr_ref[0:1, :] * (hdot(feats, w1_ref[...]) + b1_ref[...]))
        h = jnp.sin(fr_ref[1:2, :] * (hdot(h, w2_ref[...]) + b2_ref[...]))
        return hdot(h, w3_ref[...]) * jnp.exp(-feats[:, 0:1] * dl_ref[...])

    hf = taps(fa_ref[...], w3f_ref)
    hb = taps(fb_ref[...], w3b_ref)
    s = (jnp.sum(jnp.abs(hf), axis=0, keepdims=True) + jnp.sum(jnp.abs(hb), axis=0, keepdims=True)) + EPS
    o_ref[0] = hf / s
    o_ref[1] = jnp.where(lax.broadcasted_iota(I32, hb.shape, 0) == 0, 0.0, hb / s)


def hyena_taps(seq, w1, b1, w2, b2, w3, freq, a_width):
    t = jnp.arange(seq, dtype=F32) / seq
    bands = jnp.linspace(1e-4, A_BANDS - 1, A_BANDS, dtype=F32)
    ang = 2.0 * math.pi * t[:, None] * bands[None, :]
    feats = jnp.concatenate([t[:, None], jnp.cos(ang), -jnp.sin(ang)], axis=-1)
    emb = feats.shape[1]
    feats = jnp.pad(feats, ((0, 0), (0, LANE - emb)))
    feats_b = jnp.roll(feats[::-1], 1, axis=0)
    w1p = jnp.pad(w1, ((0, LANE - emb), (0, 0)))
    max_decay = math.log(A_DECAY_TARGET) / A_SHORT_DECAY_PCT
    min_decay = math.log(A_DECAY_TARGET) / A_LONG_DECAY_PCT
    deltas = jnp.abs(jnp.linspace(min_decay, max_decay, a_width, dtype=F32)).reshape(1, a_width)
    ffn = w2.shape[0]
    cb = _pick(a_width, (256, 128))
    ncb = a_width // cb
    full = lambda shape: pl.BlockSpec(shape, lambda j: (0,) * len(shape))
    taps = _call(_filter_kernel, grid=(ncb,),
                 in_specs=[full((seq, LANE)), full((seq, LANE)), full((LANE, ffn)), full((1, ffn)), full((ffn, ffn)),
                           full((1, ffn)), full((2, ffn)),
                           pl.BlockSpec((ffn, cb), lambda j: (0, j)),
                           pl.BlockSpec((ffn, cb), lambda j: (0, ncb + j)),
                           pl.BlockSpec((1, cb), lambda j: (0, j))],
                 out_specs=pl.BlockSpec((2, seq, cb), lambda j: (0, 0, j)),
                 out_shape=jax.ShapeDtypeStruct((2, seq, a_width), F32), name="hy_filter",
                 )(feats, feats_b, w1p, b1.reshape(1, ffn), w2, b2.reshape(1, ffn), freq, w3, w3, deltas)
    return taps.reshape(2 * seq, a_width)


def _hilo(x):
    x32 = np.asarray(x, np.float32)
    hi = x32.astype(ml_dtypes.bfloat16)
    lo = (x32 - hi.astype(np.float32)).astype(ml_dtypes.bfloat16)
    return hi, lo


@functools.lru_cache(maxsize=None)
def _dft_direct_consts(seq):
    n = 2 * seq
    k = np.arange(n)
    ang = 2.0 * np.pi * ((k[:, None] * k[None, :]) % n) / n
    c, s = np.cos(ang), np.sin(ang)
    fwd = np.block([[c[:, :seq], s[:, :seq]], [-s[:, :seq], c[:, :seq]]])
    taps = np.concatenate([c, -s], axis=0)
    inv = np.block([[c[:seq, :], -s[:seq, :]], [s[:seq, :], c[:seq, :]]]) / n
    return _hilo(fwd), _hilo(taps), _hilo(inv)


@functools.lru_cache(maxsize=None)
def _dft_two_stage_consts(seq):
    n = 2 * seq
    n2c = FFT_N2
    n1c = n // n2c
    h = n1c // 2
    n2 = np.arange(n2c)[:, None, None]
    k1 = np.arange(n1c)[None, :, None]
    n1 = np.arange(n1c)[None, None, :]
    psi = 2.0 * np.pi * ((n1 * k1 * n2c + n2 * k1) % n) / n
    c, s = np.cos(psi), np.sin(psi)
    m1d = np.concatenate([np.concatenate([c[:, :, :h], s[:, :, :h]], axis=2),
                          np.concatenate([-s[:, :, :h], c[:, :, :h]], axis=2)], axis=1)
    m1t = np.concatenate([c, -s], axis=1)
    ct = np.swapaxes(c, 1, 2)[:, :h, :]
    st = np.swapaxes(s, 1, 2)[:, :h, :]
    m1i = np.concatenate([np.concatenate([ct, -st], axis=2),
                          np.concatenate([st, ct], axis=2)], axis=1) / n
    kk = np.arange(n2c)
    th = 2.0 * np.pi * ((kk[:, None] * kk[None, :]) % n2c) / n2c
    c2, s2 = np.cos(th), np.sin(th)
    m2 = np.block([[c2, s2], [-s2, c2]])
    m2i = np.block([[c2, -s2], [s2, c2]])
    return _hilo(m1d), _hilo(m1t), _hilo(m1i), _hilo(m2), _hilo(m2i)


def _cmul_stacked(x, hs, half):
    xr, xi = x[:half], x[half:]
    hr, hi = hs[:half], hs[half:]
    return jnp.concatenate([xr * hr - xi * hi, xr * hi + xi * hr], axis=0)


def _const_mm3_kernel(mh_ref, ml_ref, x_ref, o_ref):
    o_ref[...] = _dot3(mh_ref[...], ml_ref[...], x_ref[...])


def const_mm3(mh, ml, x):
    m, k = mh.shape
    c = x.shape[1]
    cb = _pick(c, (256, 128))
    return _call(_const_mm3_kernel, grid=(c // cb,),
                 in_specs=[pl.BlockSpec((m, k), lambda j: (0, 0)), pl.BlockSpec((m, k), lambda j: (0, 0)),
                           pl.BlockSpec((k, cb), lambda j: (0, j))],
                 out_specs=pl.BlockSpec((m, cb), lambda j: (0, j)),
                 out_shape=jax.ShapeDtypeStruct((m, c), F32), name="dft_taps")(mh, ml, x)


def _conv_direct_kernel(fh_ref, fl_ref, gh_ref, gl_ref, zr_ref, zi_ref, h_ref, yr_ref, yi_ref, *, seq):
    x = jnp.concatenate([zr_ref[...], zi_ref[...]], axis=0)
    spec = _dot3(fh_ref[...], fl_ref[...], x)
    y = _dot3(gh_ref[...], gl_ref[...], _cmul_stacked(spec, h_ref[...], 2 * seq))
    yr_ref[...] = y[:seq]
    yi_ref[...] = y[seq:]


def long_conv_direct(z, row0, nb, seq, taps):
    a_width = z.shape[1]
    (fh, fl), (th, tl_), (gh, gl) = _dft_direct_consts(seq)
    spec_h = const_mm3(jnp.asarray(th), jnp.asarray(tl_), taps)
    cb = _pick(a_width, (256, 128))
    half = nb // 2
    b0 = row0 // seq
    full = lambda a: pl.BlockSpec(a.shape, lambda j, p: (0, 0))
    yr, yi = _call(
        functools.partial(_conv_direct_kernel, seq=seq), grid=(a_width // cb, half),
        in_specs=[full(fh), full(fl), full(gh), full(gl),
                  pl.BlockSpec((seq, cb), lambda j, p: (b0 + p, j)),
                  pl.BlockSpec((seq, cb), lambda j, p: (b0 + half + p, j)),
                  pl.BlockSpec((4 * seq, cb), lambda j, p: (0, j))],
        out_specs=[pl.BlockSpec((seq, cb), lambda j, p: (p, j))] * 2,
        out_shape=[jax.ShapeDtypeStruct((half * seq, a_width), F32)] * 2, name="conv_direct",
    )(jnp.asarray(fh), jnp.asarray(fl), jnp.asarray(gh), jnp.asarray(gl), z, z, spec_h)
    return jnp.concatenate([yr, yi], axis=0)


def _fft_s1_kernel(mh_ref, ml_ref, *refs, g, packed):
    o_ref = refs[-1]
    for j in range(g):
        if packed:
            x = jnp.concatenate([refs[0][0, j], refs[1][0, j]], axis=0)
        else:
            x = refs[0][0, j]
        o_ref[0, j] = _dot3(mh_ref[j], ml_ref[j], x)


def _fft_s1(mh, ml, zp, *, packed):
    n2c, m_rows, n1c = mh.shape
    bsz, _, rows, c = zp.shape
    p = bsz // 2 if packed else bsz
    cb = _pick(c, (256, 128))
    g = 16
    cspec = pl.BlockSpec((g, m_rows, n1c), lambda q, j, t: (t, 0, 0))
    zspecs = [pl.BlockSpec((1, g, rows, cb), lambda q, j, t: (q, t, 0, j))]
    args = [zp]
    if packed:
        zspecs.append(pl.BlockSpec((1, g, rows, cb), lambda q, j, t: (q + p, t, 0, j)))
        args.append(zp)
    return _call(functools.partial(_fft_s1_kernel, g=g, packed=packed), grid=(p, c // cb, n2c // g),
                 in_specs=[cspec, cspec] + zspecs,
                 out_specs=pl.BlockSpec((1, g, m_rows, cb), lambda q, j, t: (q, t, 0, j)),
                 out_shape=jax.ShapeDtypeStruct((p, n2c, m_rows, c), F32), name="fft_s1",
                 )(jnp.asarray(mh), jnp.asarray(ml), *args)


def _fft_s2_kernel(mh_ref, ml_ref, ih_ref, il_ref, *refs, g, conv):
    a_ref, o_ref = refs[0], refs[-1]
    half = a_ref.shape[2] // 2
    for j in range(g):
        spec = _dot3(mh_ref[...], ml_ref[...], a_ref[0, j])
        if conv:
            spec = _dot3(ih_ref[...], il_ref[...], _cmul_stacked(spec, refs[1][0, j], half))
        o_ref[0, j] = spec


def _fft_s2(m2, m2i, at, spec_h=None):
    p, n1c, rows, c = at.shape
    cb = _pick(c, (256, 128))
    g = _pick(n1c, (8, 4, 2, 1))
    conv = spec_h is not None
    mspec = pl.BlockSpec((rows, rows), lambda q, j, t: (0, 0))
    blk = pl.BlockSpec((1, g, rows, cb), lambda q, j, t: (q, t, 0, j))
    specs = [mspec] * 4 + [blk]
    args = [jnp.asarray(m2[0]), jnp.asarray(m2[1]), jnp.asarray(m2i[0]), jnp.asarray(m2i[1]), at]
    if conv:
        specs.append(pl.BlockSpec((1, g, rows, cb), lambda q, j, t: (0, t, 0, j)))
        args.append(spec_h)
    return _call(functools.partial(_fft_s2_kernel, g=g, conv=conv), grid=(p, c // cb, n1c // g),
                 in_specs=specs, out_specs=blk, out_shape=jax.ShapeDtypeStruct(at.shape, F32), name="fft_s2",
                 )(*args)


def _fft_s3_kernel(mh_ref, ml_ref, b_ref, yr_ref, yi_ref, *, g):
    half = yr_ref.shape[2]
    for j in range(g):
        y = _dot3(mh_ref[j], ml_ref[j], b_ref[0, j])
        yr_ref[0, j] = y[:half]
        yi_ref[0, j] = y[half:]


def _fft_s3(mh, ml, bt):
    n2c, n1c, rows = mh.shape
    p, _, _, c = bt.shape
    cb = _pick(c, (256, 128))
    g = 16
    h = n1c // 2
    oblk = pl.BlockSpec((1, g, h, cb), lambda q, j, t: (q, t, 0, j))
    cspec = pl.BlockSpec((g, n1c, rows), lambda q, j, t: (t, 0, 0))
    return _call(functools.partial(_fft_s3_kernel, g=g), grid=(p, c // cb, n2c // g),
                 in_specs=[cspec, cspec, pl.BlockSpec((1, g, rows, cb), lambda q, j, t: (q, t, 0, j))],
                 out_specs=[oblk, oblk],
                 out_shape=[jax.ShapeDtypeStruct((p, n2c, h, c), F32)] * 2, name="fft_s3",
                 )(jnp.asarray(mh), jnp.asarray(ml), bt)


def _swap_digits(a):
    p, x, y2, c = a.shape
    y = y2 // 2
    return a.reshape(p, x, 2, y, c).transpose(0, 3, 2, 1, 4).reshape(p, y, 2 * x, c)


def long_conv_two_stage(z, row0, nb, seq, taps):
    a_width = z.shape[1]
    m1d, m1t, m1i, m2, m2i = _dft_two_stage_consts(seq)
    n2c = FFT_N2
    n1c = 2 * seq // n2c
    taps_p = taps.reshape(1, n1c, n2c, a_width).transpose(0, 2, 1, 3)
    spec_h = _fft_s2(m2, m2i, _swap_digits(_fft_s1(m1t[0], m1t[1], taps_p, packed=False)))
    zs = lax.slice_in_dim(z, row0, row0 + nb * seq, axis=0)
    zp = zs.reshape(nb, n1c // 2, n2c, a_width).transpose(0, 2, 1, 3)
    a = _swap_digits(_fft_s1(m1d[0], m1d[1], zp, packed=True))
    b = _swap_digits(_fft_s2(m2, m2i, a, spec_h))
    yr, yi = _fft_s3(m1i[0], m1i[1], b)
    y = jnp.concatenate([yr, yi], axis=0)
    return y.transpose(0, 2, 1, 3).reshape(nb * seq, a_width)


def _hy_post_kernel(x0_ref, z_ref, bias_ref, cp_ref, cs_ref, o_ref, *, ntp):
    conv = jnp.where(pl.program_id(0) < ntp, cp_ref[...], cs_ref[...])
    o_ref[...] = (x0_ref[...] * (conv + z_ref[...] * bias_ref[...])).astype(o_ref.dtype)


def hy_post(x0, z, bias, conv_p, conv_s):
    t, a_width = x0.shape
    tl = TOK_TILE
    ntp = conv_p.shape[0] // tl
    nts = conv_s.shape[0] // tl
    cb = _pick(a_width, (512, 256, 128))
    blk = pl.BlockSpec((tl, cb), lambda i, j: (i, j))
    return _call(functools.partial(_hy_post_kernel, ntp=ntp), grid=(t // tl, a_width // cb),
                 in_specs=[blk, blk, pl.BlockSpec((1, cb), lambda i, j: (0, j)),
                           pl.BlockSpec((tl, cb), lambda i, j: (jnp.minimum(i, ntp - 1), j)),
                           pl.BlockSpec((tl, cb), lambda i, j: (jnp.clip(i - ntp, 0, nts - 1), j))],
                 out_specs=blk, out_shape=jax.ShapeDtypeStruct((t, a_width), BF16), name="hy_post",
                 )(x0, z, bias.reshape(1, a_width), conv_p, conv_s)


def _qkv_prep_kernel(q_ref, k_ref, v_ref, qn_ref, kn_ref, cos_ref, sin_ref, qo_ref, ko_ref, vo_ref, kc_ref, *, hd, scale):
    cos = cos_ref[...]
    sin = sin_ref[...]
    lane = lax.broadcasted_iota(I32, cos.shape, 1)
    low = (lane % (hd // 2)) < (hd // 4)

    def norm(x, g):
        return x * lax.rsqrt(jnp.mean(x * x, axis=-1, keepdims=True) + EPS) * g

    def rope(x):
        partner = jnp.where(low, pltpu.roll(x, hd - hd // 4, 1), pltpu.roll(x, hd // 4, 1))
        return x * cos + partner * sin

    for h in range(q_ref.shape[1] // hd):
        sl = slice(h * hd, (h + 1) * hd)
        qo_ref[:, sl] = (rope(norm(q_ref[:, sl], qn_ref[...])) * scale).astype(qo_ref.dtype)
    for h in range(k_ref.shape[1] // hd):
        sl = slice(h * hd, (h + 1) * hd)
        kn = norm(k_ref[:, sl], kn_ref[...])
        kc_ref[:, sl] = kn
        ko_ref[:, sl] = rope(kn).astype(ko_ref.dtype)
    vo_ref[...] = v_ref[...].astype(vo_ref.dtype)


def qkv_prep(u, col0, n_heads, n_kv, hd, qn, kn, cos_t, sin_t):
    t = u.shape[0]
    tl = TOK_TILE
    qw, kw = n_heads * hd, n_kv * hd
    assert col0 % qw == 0 and (col0 + qw) % kw == 0
    qb, kb = col0 // qw, (col0 + qw) // kw
    row = lambda w: pl.BlockSpec((tl, w), lambda i: (i, 0))
    vec = pl.BlockSpec((1, hd), lambda i: (0, 0))
    return _call(functools.partial(_qkv_prep_kernel, hd=hd, scale=hd ** -0.5), grid=(t // tl,),
                 in_specs=[pl.BlockSpec((tl, qw), lambda i: (i, qb)),
                           pl.BlockSpec((tl, kw), lambda i: (i, kb)),
                           pl.BlockSpec((tl, kw), lambda i: (i, kb + 1)),
                           vec, vec, row(hd), row(hd)],
                 out_specs=[row(qw), row(kw), row(kw), row(kw)],
                 out_shape=[jax.ShapeDtypeStruct((t, qw), BF16), jax.ShapeDtypeStruct((t, kw), BF16),
                            jax.ShapeDtypeStruct((t, kw), BF16), jax.ShapeDtypeStruct((t, kw), F32)],
                 name="qkv_prep")(u, u, u, qn.reshape(1, hd), kn.reshape(1, hd), cos_t, sin_t)


def _rope_tables(seq, hd):
    rows = seq // GRID_W
    row = jnp.repeat(jnp.arange(rows), GRID_W).astype(F32)
    col = jnp.tile(jnp.arange(GRID_W), rows).astype(F32)
    nf = hd // 4
    inv = jnp.exp(-math.log(ROPE_BASE) * jnp.arange(nf, dtype=F32) / nf)
    ang_r = row[:, None] * inv[None, :]
    ang_c = col[:, None] * inv[None, :]
    cos = jnp.concatenate([jnp.cos(ang_r)] * 2 + [jnp.cos(ang_c)] * 2, axis=-1)
    sin = jnp.concatenate([-jnp.sin(ang_r), jnp.sin(ang_r), -jnp.sin(ang_c), jnp.sin(ang_c)], axis=-1)
    return cos, sin


def _attn_kernel(sink_ref, q_ref, *refs, kinds, group, hd, nblk):
    nk = len(kinds)
    k_refs, v_refs, o_ref = refs[:nk], refs[nk:2 * nk], refs[2 * nk]
    kvg = pl.program_id(1)
    i = pl.program_id(2)
    tq = q_ref.shape[0]
    q = jnp.concatenate([q_ref[:, g * hd:(g + 1) * hd] for g in range(group)], axis=0)
    rows = lax.broadcasted_iota(I32, (group * tq, 1), 0)
    sink = jnp.full((group * tq, 1), sink_ref[kvg * group], F32)
    for g in range(1, group):
        sink = jnp.where(rows >= g * tq, sink_ref[kvg * group + g], sink)

    def load(ref):
        x = ref[0] if len(ref.shape) == 3 else ref[...]
        return x.astype(BF16)

    scores = []
    m = sink
    for kind, k_ref in zip(kinds, k_refs):
        s = lax.dot_general(q, load(k_ref), (((1,), (1,)), ((), ())), preferred_element_type=F32)
        if kind in ("prev", "next"):
            r = lax.broadcasted_iota(I32, s.shape, 0) % tq
            c = lax.broadcasted_iota(I32, s.shape, 1)
            if kind == "prev":
                ok = c >= r + jnp.where(i > 0, 0, 2 * tq)
            else:
                ok = c <= r - jnp.where(i < nblk - 1, 0, 2 * tq)
            s = jnp.where(ok, s, NEG_INF)
        scores.append(s)
        m = jnp.maximum(m, jnp.max(s, axis=-1, keepdims=True))
    den = jnp.exp(sink - m)
    acc = None
    for s, v_ref in zip(scores, v_refs):
        p = jnp.exp(s - m)
        den = den + jnp.sum(p, axis=-1, keepdims=True)
        pv = _bdot(p.astype(BF16), load(v_ref))
        acc = pv if acc is None else acc + pv
    out = acc / den
    o_ref[...] = jnp.concatenate([out[g * tq:(g + 1) * tq] for g in range(group)], axis=1).astype(o_ref.dtype)


def attn_context(q, k, v, sink, row0, nb, seq, n_kv, hd):
    group = q.shape[1] // (n_kv * hd)
    b0 = row0 // seq
    kv = pl.BlockSpec((seq, hd), lambda b, g, i, s: (b0 + b, g))
    return _call(functools.partial(_attn_kernel, kinds=("all",), group=group, hd=hd, nblk=1),
                 grid=(nb, n_kv, 1), nsp=1,
                 in_specs=[pl.BlockSpec((seq, group * hd), lambda b, g, i, s: (b0 + b, g)), kv, kv],
                 out_specs=pl.BlockSpec((seq, group * hd), lambda b, g, i, s: (b, g)),
                 out_shape=jax.ShapeDtypeStruct((nb * seq, q.shape[1]), BF16), name="attn_context",
                 )(sink, q, k, v)


def attn_latent(q, k, v, ck, cv, sink, row0, nb, seq, n_kv, hd):
    group = q.shape[1] // (n_kv * hd)
    blk = ATT_BLOCK
    nblk = seq // blk
    r0 = row0 // blk
    past = ck.shape[1]
    cur = lambda b, g, i, s: (r0 + b * nblk + i, g)
    prev = lambda b, g, i, s: (r0 + b * nblk + jnp.maximum(i - 1, 0), g)
    nxt = lambda b, g, i, s: (r0 + b * nblk + jnp.minimum(i + 1, nblk - 1), g)
    ctx = pl.BlockSpec((1, past, hd), lambda b, g, i, s: (b, 0, g))
    kvs = [pl.BlockSpec((blk, hd), f) for f in (prev, cur, nxt)]
    return _call(functools.partial(_attn_kernel, kinds=("prev", "cur", "next", "ctx"), group=group, hd=hd, nblk=nblk),
                 grid=(nb, n_kv, nblk), nsp=1,
                 in_specs=[pl.BlockSpec((blk, group * hd), cur)] + kvs + [ctx] + kvs + [ctx],
                 out_specs=pl.BlockSpec((blk, group * hd), lambda b, g, i, s: (b * nblk + i, g)),
                 out_shape=jax.ShapeDtypeStruct((nb * seq, q.shape[1]), BF16), name="attn_latent",
                 )(sink, q, k, k, k, ck, v, v, v, cv)


def _ret_kernel(lg_ref, q_ref, kt_ref, v_ref, g_ref, gn_ref, *refs, ch, nc, dk, has_init, emit_state):
    pos = 0
    if has_init:
        s0_refs = refs[0:2]
        pos = 2
    y_ref = refs[pos]
    pos += 1
    if emit_state:
        so_refs = refs[pos:pos + 2]
        pos += 2
    o_scr, s_scr = refs[pos], refs[pos + 1]
    h = pl.program_id(1)
    kscale = dk ** -0.5
    ii = lax.broadcasted_iota(I32, (ch, ch), 0).astype(F32)
    jj = lax.broadcasted_iota(I32, (ch, ch), 1).astype(F32)
    col = lax.broadcasted_iota(I32, (ch, 1), 0).astype(F32)
    lane = lax.broadcasted_iota(I32, (1, ch), 1).astype(F32)
    gn = gn_ref[0]

    consts = []
    for d in range(2):
        lg = lg_ref[d, h]
        diff = ii - jj if d == 0 else jj - ii
        dmat = jnp.where(diff >= 0, jnp.exp(lg * jnp.maximum(diff, 0.0)), 0.0) * kscale
        xi = jnp.exp(lg * (col + 1.0)) if d == 0 else jnp.exp(lg * (ch - col))
        zeta = (jnp.exp(lg * (ch - 1.0 - lane)) if d == 0 else jnp.exp(lg * lane)) * kscale
        cdec = jnp.exp(lg * jnp.full((1, 1), float(ch), F32))
        consts.append((dmat, xi, zeta, cdec))
        if has_init:
            s_scr[d] = s0_refs[d][0, 0, 0]
        else:
            s_scr[d] = jnp.zeros(s_scr.shape[1:], F32)

    def chunk_out(d, c):
        dmat, xi, zeta, cdec = consts[d]
        r0 = pl.multiple_of(c * ch, ch)
        qc = q_ref[0, pl.ds(r0, ch), :]
        ktc = kt_ref[:, pl.ds(r0, ch)]
        vc = v_ref[0, pl.ds(r0, ch), :]
        sc = _bdot(qc, ktc) * dmat
        s = s_scr[d]
        o = _bdot(sc.astype(BF16), vc) + _bdot(qc, s.astype(BF16)) * xi
        kz = (ktc.astype(F32) * zeta).astype(BF16)
        s_scr[d] = s * cdec + _bdot(kz, vc)
        return r0, o

    def finalize(r0, o):
        mu = jnp.mean(o, axis=-1, keepdims=True)
        var = jnp.mean(jnp.square(o - mu), axis=-1, keepdims=True)
        on = (o - mu) * lax.rsqrt(var + EPS) * gn
        gate = _silu(g_ref[0, pl.ds(r0, ch), :].astype(F32))
        y_ref[0, pl.ds(r0, ch), :] = (gate * on).astype(y_ref.dtype)

    def first_half(n, carry):
        for d in range(2):
            r0, o = chunk_out(d, n if d == 0 else nc - 1 - n)
            o_scr[pl.ds(r0, ch), :] = o
        return carry

    def second_half(n, carry):
        for d in range(2):
            r0, o = chunk_out(d, n if d == 0 else nc - 1 - n)
            finalize(r0, o_scr[pl.ds(r0, ch), :] + o)
        return carry

    lax.fori_loop(0, nc // 2, first_half, 0)
    lax.fori_loop(nc // 2, nc, second_half, 0)
    if emit_state:
        for d in range(2):
            so_refs[d][0, 0, 0] = s_scr[d]


def retention(q, kt, vg, row0, nb, seq, n_heads, dk, dv, log_gamma, gn, *, s_f0=None, s_b0=None, emit_state=False):
    t = q.shape[0]
    q3 = q.reshape(t // seq, seq, n_heads * dk)
    vg3 = vg.reshape(t // seq, seq, 2 * n_heads * dv)
    b0 = row0 // seq
    has_init = s_f0 is not None
    specs = [pl.BlockSpec(memory_space=pltpu.SMEM),
             pl.BlockSpec((1, seq, dk), lambda b, h: (b0 + b, 0, h)),
             pl.BlockSpec((dk, seq), lambda b, h: (h, b0 + b)),
             pl.BlockSpec((1, seq, dv), lambda b, h: (b0 + b, 0, h)),
             pl.BlockSpec((1, seq, dv), lambda b, h: (b0 + b, 0, n_heads + h)),
             pl.BlockSpec((1, 1, dv), lambda b, h: (h, 0, 0))]
    args = [log_gamma, q3, kt, vg3, vg3, gn.reshape(n_heads, 1, dv)]
    st = pl.BlockSpec((1, 1, 1, dk, dv), lambda b, h: (b, 0, h, 0, 0))
    if has_init:
        specs += [st, st]
        args += [s_f0, s_b0]
    out_specs = [pl.BlockSpec((1, seq, dv), lambda b, h: (b, 0, h))]
    out_shape = [jax.ShapeDtypeStruct((nb, seq, n_heads * dv), BF16)]
    if emit_state:
        out_specs += [st, st]
        out_shape += [jax.ShapeDtypeStruct((nb, 1, n_heads, dk, dv), F32)] * 2
    gs = pl.GridSpec(grid=(nb, n_heads), in_specs=specs, out_specs=out_specs,
                     scratch_shapes=[pltpu.VMEM((seq, dv), F32), pltpu.VMEM((2, dk, dv), F32)])
    ch = 2 * RET_CHUNK if seq % (4 * RET_CHUNK) == 0 else RET_CHUNK
    outs = pl.pallas_call(
        functools.partial(_ret_kernel, ch=ch, nc=seq // ch, dk=dk, has_init=has_init, emit_state=emit_state),
        grid_spec=gs, out_shape=out_shape, name="retention",
        compiler_params=pltpu.CompilerParams(dimension_semantics=("arbitrary", "arbitrary"),
                                             vmem_limit_bytes=VMEM_LIMIT))(*args)
    y = outs[0].reshape(nb * seq, n_heads * dv)
    return (y,) + tuple(outs[1:])


def _ffn_up_kernel(idx_ref, h_hbm, w1_ref, w3_ref, o_ref, xf_scr, xb_scr, sem, *, tm, nf):
    tile = pl.program_id(0)
    f = pl.program_id(1)
    slot = tile % 2
    per = tm // nf

    def row_copy(row, dst_slot, r):
        return pltpu.make_async_copy(h_hbm.at[pl.ds(row, 1)], xf_scr.at[dst_slot, pl.ds(r, 1)], sem.at[dst_slot])

    def wait_rows(src_slot):
        pltpu.make_async_copy(h_hbm.at[pl.ds(0, tm)], xf_scr.at[src_slot], sem.at[src_slot]).wait()

    @pl.when(jnp.logical_and(tile == 0, f == 0))
    def _():
        def issue(r, carry):
            row_copy(idx_ref[r], 0, r).start()
            return carry

        lax.fori_loop(0, tm, issue, 0)

    @pl.when(f == 0)
    def _():
        wait_rows(slot)
        xb_scr[...] = xf_scr[slot].astype(BF16)

    base = (tile + 1) * tm + f * per
    for r in range(per):
        row_copy(idx_ref[base + r], 1 - slot, f * per + r).start()

    x = xb_scr[...]
    a = _bdot(x, w1_ref[0, 0].astype(BF16))
    b = _bdot(x, w3_ref[0, 0].astype(BF16))
    o_ref[...] = (_silu(a) * b).astype(o_ref.dtype)

    @pl.when(jnp.logical_and(tile == pl.num_programs(0) - 1, f == nf - 1))
    def _():
        wait_rows(1 - slot)


def ffn_up(h, rows, w1, w3, layer, tm, tiles_per_expert):
    _, n_exp, d, ff = w1.shape
    nslots = rows.shape[0]
    tf = _pick(ff, (512, 256, 128))
    nf = ff // tf
    tpe = tiles_per_expert
    rows = jnp.pad(rows, (0, tm))
    wspec = pl.BlockSpec((1, 1, d, tf), lambda t, f, idx: (layer, t // tpe, 0, f))
    return _call(functools.partial(_ffn_up_kernel, tm=tm, nf=nf), grid=(nslots // tm, nf), nsp=1,
                 in_specs=[pl.BlockSpec(memory_space=pl.ANY), wspec, wspec],
                 out_specs=pl.BlockSpec((tm, tf), lambda t, f, idx: (t, f)),
                 out_shape=jax.ShapeDtypeStruct((nslots, ff), BF16),
                 scratch=[pltpu.VMEM((2, tm, d), F32), pltpu.VMEM((tm, d), BF16), pltpu.SemaphoreType.DMA((2,))],
                 name="ffn_up")(rows, h, w1, w3)


def _ffn_down_kernel(x_ref, w_ref, g_ref, o_ref, wb):
    @pl.when(pl.program_id(2) == 0)
    def _():
        wb[...] = w_ref[0, 0].astype(BF16)

    acc = _bdot(x_ref[...], wb[...])
    g = g_ref[...]
    for c in range(acc.shape[1] // LANE):
        o_ref[:, c * LANE:(c + 1) * LANE] = acc[:, c * LANE:(c + 1) * LANE] * g


def ffn_down(hm, w2, layer, gates, tm, tiles_per_expert):
    _, n_exp, ff, d = w2.shape
    nslots = hm.shape[0]
    tn = _pick(d, (512, 256, 128))
    tpe = tiles_per_expert
    return _call(_ffn_down_kernel, grid=(n_exp, d // tn, tpe),
                 in_specs=[pl.BlockSpec((tm, ff), lambda e, j, r: (e * tpe + r, 0)),
                           pl.BlockSpec((1, 1, ff, tn), lambda e, j, r: (layer, e, 0, j)),
                           pl.BlockSpec((tm, LANE), lambda e, j, r: (e * tpe + r, 0))],
                 out_specs=pl.BlockSpec((tm, tn), lambda e, j, r: (e * tpe + r, j)),
                 out_shape=jax.ShapeDtypeStruct((nslots, d), F32),
                 scratch=[pltpu.VMEM((ff, tn), BF16)], name="ffn_down")(hm, w2, gates)


SLOT_GROUP = 8
COMBINE_BATCH = 32
COMBINE_TILE = 512


def _combine_kernel(flat_ref, pstart_ref, cidx_ref, y_hbm, t_hbm, x_ref, g_ref, *refs, nsplit):
    nout = 1 if nsplit is None else 2
    o_refs = refs[:nout]
    z_scr, tz_scr, acc_scr, st_ref, sem = refs[nout:]
    i = pl.program_id(0)
    tl = x_ref.shape[0]
    gb = COMBINE_BATCH
    rows_b = gb * SLOT_GROUP
    p0, p1 = pstart_ref[i], pstart_ref[i + 1]
    npos = p1 - p0
    nbatch = (npos + gb - 1) // gb
    next_has = jnp.logical_and(i + 1 < pl.num_programs(0), pstart_ref[i + 2] > p1)

    @pl.when(i == 0)
    def _():
        st_ref[0] = 0
        st_ref[1] = 0

    s0 = st_ref[0]
    started = st_ref[1]

    def gather(pbase, slot):
        def issue(k, carry):
            r = pl.multiple_of(flat_ref[pbase + k] * SLOT_GROUP, SLOT_GROUP)
            ko = pl.multiple_of(k * SLOT_GROUP, SLOT_GROUP)
            pltpu.make_async_copy(y_hbm.at[pl.ds(r, SLOT_GROUP)], z_scr.at[slot, pl.ds(ko, SLOT_GROUP)],
                                  sem.at[slot]).start()
            pltpu.make_async_copy(t_hbm.at[pl.ds(r, SLOT_GROUP)], tz_scr.at[slot, pl.ds(ko, SLOT_GROUP)],
                                  sem.at[slot]).start(priority=1)
            return carry

        lax.fori_loop(0, gb, issue, 0, unroll=8)

    @pl.when(jnp.logical_and(nbatch > 0, started == 0))
    def _():
        gather(p0, s0)

    acc_scr[...] = jnp.zeros_like(acc_scr)
    t_col = (i * tl + lax.broadcasted_iota(I32, (tl, 1), 0)).astype(F32)
    lane_j = lax.broadcasted_iota(I32, (1, rows_b), 1)

    def batch(bi, carry):
        slot = (s0 + bi) % 2

        @pl.when(bi + 1 < nbatch)
        def _():
            gather(p0 + (bi + 1) * gb, 1 - slot)

        @pl.when(jnp.logical_and(bi + 1 == nbatch, next_has))
        def _():
            gather(p1, 1 - slot)

        pltpu.make_async_copy(y_hbm.at[pl.ds(0, rows_b)], z_scr.at[slot], sem.at[slot]).wait()
        pltpu.make_async_copy(t_hbm.at[pl.ds(0, rows_b)], tz_scr.at[slot], sem.at[slot]).wait()
        tok_row = tz_scr[slot].T[0:1, :]
        tok_row = jnp.where(lane_j < (npos - bi * gb) * SLOT_GROUP, tok_row, -1.0)
        onehot = jnp.where(tok_row == t_col, 1.0, 0.0).astype(BF16)
        acc_scr[...] += _bdot(onehot, z_scr[slot].astype(BF16))
        return carry

    lax.fori_loop(0, nbatch, batch, 0)

    @pl.when(nbatch > 0)
    def _():
        st_ref[0] = (s0 + nbatch) % 2
        st_ref[1] = next_has.astype(I32)

    @pl.when(nbatch == 0)
    def _():
        st_ref[1] = 0

    out = x_ref[...] + g_ref[0, 0] * acc_scr[...]
    if nsplit is None:
        o_refs[0][...] = out
    else:
        @pl.when(i < nsplit)
        def _():
            o_refs[0][...] = out

        @pl.when(i >= nsplit)
        def _():
            o_refs[1][...] = out


def moe_combine(x, y, toks, flat, pstart, mods, cidx, k_gate, split=None):
    t, d = x.shape
    tl = COMBINE_TILE
    stride = tl // TOK_TILE
    blk = pl.BlockSpec((tl, d), lambda i, *_: (i, 0))
    if split is None:
        nsplit, out_specs, out_shape = None, blk, jax.ShapeDtypeStruct((t, d), F32)
    else:
        nsplit, n1 = split // tl, (t - split) // tl
        out_specs = [pl.BlockSpec((tl, d), lambda i, *_: (jnp.minimum(i, nsplit - 1), 0)),
                     pl.BlockSpec((tl, d), lambda i, *_: (jnp.clip(i - nsplit, 0, n1 - 1), 0))]
        out_shape = [jax.ShapeDtypeStruct((split, d), F32), jax.ShapeDtypeStruct((t - split, d), F32)]
    rows_b = COMBINE_BATCH * SLOT_GROUP
    return _call(functools.partial(_combine_kernel, nsplit=nsplit), grid=(t // tl,), nsp=3,
                 in_specs=[pl.BlockSpec(memory_space=pl.ANY), pl.BlockSpec(memory_space=pl.ANY), blk,
                           pl.BlockSpec((1, 1, 1, d), lambda i, f, p, c: (c[i * stride], k_gate, 0, 0))],
                 out_specs=out_specs, out_shape=out_shape,
                 scratch=[pltpu.VMEM((2, rows_b, d), F32), pltpu.VMEM((2, rows_b, LANE), F32),
                          pltpu.VMEM((tl, d), F32), pltpu.SMEM((2,), I32), pltpu.SemaphoreType.DMA((2,))],
                 name="moe_combine")(flat, pstart, cidx, y, toks, x, mods)


def _route(aff_t, groups):
    n_exp = aff_t.shape[0]
    caps = [CAPACITY_FACTOR * seq // n_exp for _, _, seq in groups]
    slots = sum(nb * cap for (_, nb, _), cap in zip(groups, caps))
    e_ids = jnp.arange(n_exp, dtype=I32)[:, None, None]
    rows, gates, g0s, ngs = [], [], [], []
    off = 0
    for (row0, nb, seq), cap in zip(groups, caps):
        a = lax.slice_in_dim(aff_t, row0, row0 + nb * seq, axis=1).reshape(n_exp, nb, seq)
        gate, idx = lax.top_k(a, cap)
        idx, gate = lax.sort((idx.astype(I32), gate), dimension=2, num_keys=1)
        b_ids = jnp.arange(nb, dtype=I32)[None, :, None]
        gtok = (b_ids * seq + idx).reshape(n_exp, nb * cap)
        rows.append(row0 + gtok)
        gates.append(gate.reshape(n_exp, nb * cap))
        ntile = nb * seq // COMBINE_TILE
        bounds = jnp.arange(ntile + 1, dtype=I32) * COMBINE_TILE
        cnt = jnp.sum(gtok[:, None, :] < bounds[None, :, None], axis=-1, dtype=I32)
        base = e_ids[:, :, 0] * slots + off
        lo, hi = base + cnt[:, :-1], base + cnt[:, 1:]
        g0 = lo // SLOT_GROUP
        ng = jnp.where(hi > lo, (hi + SLOT_GROUP - 1) // SLOT_GROUP - g0, 0)
        g0s.append(g0.T)
        ngs.append(ng.T)
        off += nb * cap
    g0 = jnp.concatenate(g0s, axis=0).reshape(-1)
    ng = jnp.concatenate(ngs, axis=0).reshape(-1)
    ntiles = g0.shape[0] // n_exp
    cum = jnp.cumsum(ng)
    start = cum - ng
    pstart = jnp.concatenate([start[::n_exp], cum[-1:], cum[-1:]]).astype(I32)
    pmax = n_exp * slots // SLOT_GROUP + 2 * ntiles * n_exp + COMBINE_BATCH
    p = jnp.arange(pmax, dtype=I32)
    delta = g0 - start
    step = delta - jnp.concatenate([jnp.zeros((1,), I32), delta[:-1]])
    offs = jnp.sum(jnp.where(p[:, None] >= start[None, :], step[None, :], 0), axis=1, dtype=I32)
    flat = jnp.where(p < cum[-1], p + offs, 0).astype(I32)
    return jnp.concatenate(rows, axis=1), jnp.concatenate(gates, axis=1), flat, pstart


def ec_moe(x, gain, mods, cidx, router, w1, w3, w2, layer, groups, split=None):
    n_exp = router.shape[1]
    h, aff_t = norm_mod(x, gain, mods, cidx, 3, 4, router_t=router.T)
    rows, gates, flat, pstart = _route(aff_t, groups)
    slots = rows.shape[1]
    tm = _pick(slots, (1024, 512, 256, 128, 64, 32, 16, 8))
    tpe = slots // tm
    rows_f = rows.reshape(-1)
    hm = ffn_up(h, rows_f, w1, w3, layer, tm, tpe)
    lanes = lambda v: jnp.broadcast_to(v.reshape(-1, 1), (n_exp * slots, LANE))
    tm_down = _pick(slots, (2560, 1280, 1024, 512, 256, 128, 64, 32, 16, 8))
    y = ffn_down(hm, w2, layer, lanes(gates), tm_down, slots // tm_down)
    return moe_combine(x, y, lanes(rows_f.astype(F32)), flat, pstart, mods, cidx, 5, split)


def kernel(x_prompt, x_sample, cache_attn_k, cache_attn_v, state_ret_fwd, state_ret_bwd, c, c_ctx, ada_w, ada_b, norm1_g, norm2_g, ev_w_in, ev_w_out, hy_sconv_w, hy_sconv_b, hy_ffn_w1, hy_ffn_b1, hy_ffn_w2, hy_ffn_b2, hy_ffn_w3, hy_freq, hy_bias, at_q_norm, at_k_norm, at_sink, od_w_in, od_w_out, ret_decay, ret_gn, moe_router, moe_w1, moe_w3, moe_w2):
    bp, lp, d = x_prompt.shape
    bs, ls, _ = x_sample.shape
    depth = ada_w.shape[0]
    tp, ts = bp * lp, bs * ls
    t = tp + ts
    groups = ((0, bp, lp), (tp, bs, ls))
    a_width = hy_bias.shape[-1]
    hd = at_q_norm.shape[-1]
    n_heads = at_sink.shape[-1]
    n_kv = cache_attn_k.shape[3]
    c_heads, c_dv = ret_gn.shape[1], ret_gn.shape[2]
    c_dk = (od_w_in.shape[-1] - 2 * c_heads * c_dv) // (2 * c_heads)

    ntp, nts = tp // TOK_TILE, ts // TOK_TILE
    tps, tss = lp // TOK_TILE, ls // TOK_TILE
    tile = np.arange(ntp + nts)
    cidx = jnp.asarray(np.where(tile < ntp, 0, 1 + (tile - ntp) // tss), I32)
    in_seq = np.where(tile < ntp, tile % tps, (tile - ntp) % tss)
    seq_tiles = np.where(tile < ntp, tps, tss)
    first = jnp.asarray(in_seq == 0, I32)
    last = jnp.asarray(in_seq == seq_tiles - 1, I32)
    tm_mm = _pick(math.gcd(tp, ls), (1024, 512, 256))
    tm_big = _pick(t, (2048, 1024, 512, 256))

    cond = jnp.concatenate([c_ctx[None, :], c], axis=0)
    cond = jnp.pad(cond, ((0, (-cond.shape[0]) % 8), (0, 0)))
    mods_all = adaln_all(cond, ada_w, ada_b)
    x = (x_prompt.reshape(tp, d), x_sample.reshape(ts, d))

    new_k, new_v, new_sf, new_sb = [], [], [], []
    for l in range(depth):
        mods = mods_all[l].reshape(-1, 6, 1, d)
        j = l // 2
        h = norm_mod(x, norm1_g[l], mods, cidx, 0, 1)
        if l % 2 == 0:
            u = matmul([h], ev_w_in[j], out_dtype=F32, tm=tm_big)
            z, x0 = hy_pre(u, hy_sconv_w[j], hy_sconv_b[j], first, last, a_width)
            convs = []
            for row0, nb, seq in groups:
                taps = hyena_taps(seq, hy_ffn_w1[j], hy_ffn_b1[j], hy_ffn_w2[j], hy_ffn_b2[j],
                                  hy_ffn_w3[j], hy_freq[j], a_width)
                conv_fn = long_conv_direct if seq <= 512 else long_conv_two_stage
                convs.append(conv_fn(z, row0, nb, seq, taps))
            ya = hy_post(x0, z, hy_bias[j], convs[0], convs[1])
            cos_s, sin_s = _rope_tables(ls, hd)
            cos_t = jnp.concatenate([jnp.ones((tp, hd), F32), jnp.tile(cos_s, (bs, 1))], axis=0)
            sin_t = jnp.concatenate([jnp.zeros((tp, hd), F32), jnp.tile(sin_s, (bs, 1))], axis=0)
            q, k, v, k_normed = qkv_prep(u, 3 * a_width, n_heads, n_kv, hd, at_q_norm[j], at_k_norm[j], cos_t, sin_t)
            yb_p = attn_context(q, k, v, at_sink[j], 0, bp, lp, n_kv, hd)
            past = cache_attn_k.shape[2]
            ck = cache_attn_k[:, j].reshape(bs, past, n_kv * hd)
            cv = cache_attn_v[:, j].reshape(bs, past, n_kv * hd)
            yb_s = attn_latent(q, k, v, ck, cv, at_sink[j], tp, bs, ls, n_kv, hd)
            x = matmul([ya, (yb_p, yb_s)], ev_w_out[j], out_dtype=F32, tm=tm_mm, res=x, mods=mods, cidx=cidx, k_gate=2)
            kv0 = 3 * a_width + n_heads * hd
            new_k.append(k_normed[:tp].reshape(bp, 1, lp, n_kv, hd))
            new_v.append(u[:tp, kv0 + n_kv * hd:kv0 + 2 * n_kv * hd].reshape(bp, 1, lp, n_kv, hd))
        else:
            nqk = c_heads * c_dk
            w_in = od_w_in[j]
            q = matmul([h], w_in, out_dtype=BF16, tm=tm_big, ncols=nqk)
            kt = matmul_t(h, w_in[:, nqk:2 * nqk].T, out_dtype=BF16, tm=tm_big)
            vg = matmul([h], w_in, out_dtype=BF16, tm=tm_big, col0=2 * nqk)
            log_gamma = -jnp.exp(ret_decay[j].astype(F32))
            y_p, sf, sb = retention(q, kt, vg, 0, bp, lp, c_heads, c_dk, c_dv, log_gamma, ret_gn[j], emit_state=True)
            (y_s,) = retention(q, kt, vg, tp, bs, ls, c_heads, c_dk, c_dv, log_gamma, ret_gn[j],
                               s_f0=state_ret_fwd[:, j:j + 1], s_b0=state_ret_bwd[:, j:j + 1])
            x = matmul([(y_p, y_s)], od_w_out[j], out_dtype=F32, tm=min(tm_mm, 512), res=x, mods=mods, cidx=cidx,
                       k_gate=2)
            new_sf.append(sf)
            new_sb.append(sb)
        x = ec_moe(x, norm2_g[l], mods, cidx, moe_router[l], moe_w1, moe_w3, moe_w2, l, groups,
                   split=tp if l == depth - 1 else None)

    cat = lambda parts: jnp.concatenate(parts, axis=1)
    return (x[0].reshape(bp, lp, d), x[1].reshape(bs, ls, d), cat(new_k), cat(new_v), cat(new_sf), cat(new_sb))
```
